```python
import math
import jax, jax.numpy as jnp
from jax import lax
import numpy as np

D_MODEL = 1024
BATCH = 16
SEQ = 4096
DEPTH = 4

CHUNK = 64
N_META = 16
Q_BLOCK = 128
DA_HEADS = D_MODEL // 128
DA_HEAD_DIM = 32
DA_WIDTH = DA_HEADS * 2 * DA_HEAD_DIM
LRU_WIDTH = D_MODEL
LRU_HEADS = D_MODEL // 64
LRU_BLOCK = LRU_WIDTH // LRU_HEADS
CONV_WIDTH = 4
LRU_C = 8.0
D_FF = 256 * ((8 * D_MODEL // 3 + 255) // 256)
N_EXPERTS = 8
TOP_K = 2
N_DENSE = (DEPTH + 1) // 2
N_MOE = DEPTH // 2
ALPHA = (2 * DEPTH) ** 0.25
BETA = (8 * DEPTH) ** -0.25
LN_EPS = 1e-5
RMS_EPS = 1e-5
PROJ_WIDTH = 3 * DA_WIDTH + 2 * LRU_WIDTH + 2 * D_MODEL
PROJ_SPLITS = (DA_WIDTH, 2 * DA_WIDTH, 3 * DA_WIDTH, 3 * DA_WIDTH + LRU_WIDTH,
               3 * DA_WIDTH + 2 * LRU_WIDTH, 3 * DA_WIDTH + 2 * LRU_WIDTH + D_MODEL)

kernel_name = 'hybrid_diffattn_rglru_moe_deepnorm'


def layer_norm(x, g, b):
    xf = x.astype(jnp.float32)
    mu = jnp.mean(xf, axis=-1, keepdims=True)
    var = jnp.mean(jnp.square(xf - mu), axis=-1, keepdims=True)
    y = (xf - mu) * lax.rsqrt(var + LN_EPS)
    return (y * g.astype(jnp.float32) + b.astype(jnp.float32)).astype(x.dtype)


def diff_attention(q, k, v, lam, lam_init, subln_g):
    B, T = q.shape[0], q.shape[1]
    q = q * (DA_HEAD_DIM ** -0.5)

    def attend(qb, kb, vb, mask):
        s = jnp.einsum('bqhmd,bkhmd->bhmqk', qb, kb).astype(jnp.float32)
        if mask is not None:
            s = jnp.where(mask, s, -jnp.inf)
        p = jax.nn.softmax(s, axis=-1)
        a = p[:, :, 0] - lam * p[:, :, 1]
        return jnp.einsum('bhqk,bkhe->bqhe', a.astype(vb.dtype), vb)

    outs = [attend(q[:, :N_META], k[:, :N_META], v[:, :N_META], None)]
    cidx = np.arange(Q_BLOCK) // CHUNK
    diag = jnp.asarray(cidx[None, :] <= cidx[:, None])
    n_blocks = (T - N_META) // Q_BLOCK
    for blk in range(n_blocks):
        q0 = N_META + blk * Q_BLOCK
        k_end = q0 + Q_BLOCK
        mask = jnp.concatenate([jnp.ones((Q_BLOCK, q0), dtype=bool), diag], axis=1)
        outs.append(attend(q[:, q0:k_end], k[:, :k_end], v[:, :k_end], mask))
    o = jnp.concatenate(outs, axis=1)
    of = o.astype(jnp.float32)
    of = of * lax.rsqrt(jnp.mean(jnp.square(of), axis=-1, keepdims=True) + RMS_EPS)
    of = of * subln_g.astype(jnp.float32) * (1.0 - lam_init)
    return of.astype(o.dtype).reshape(B, T, DA_WIDTH)


def causal_depthwise_conv(x, w, b):
    T = x.shape[1]
    xp = jnp.pad(x, ((0, 0), (CONV_WIDTH - 1, 0), (0, 0)))
    y = xp[:, 0:T] * w[0]
    for j in range(1, CONV_WIDTH):
        y = y + xp[:, j:j + T] * w[j]
    return y + b


def rg_lru(x, w_r, b_r, w_i, b_i, lam_param):
    B, T, C = x.shape
    xb = x.reshape(B, T, LRU_HEADS, LRU_BLOCK)
    r = jax.nn.sigmoid((jnp.einsum('bthi,hij->bthj', xb, w_r).reshape(B, T, C) + b_r).astype(jnp.float32))
    i = jax.nn.sigmoid((jnp.einsum('bthi,hij->bthj', xb, w_i).reshape(B, T, C) + b_i).astype(jnp.float32))
    log_a = -LRU_C * r * jax.nn.softplus(-lam_param.astype(jnp.float32))
    a = jnp.exp(log_a)
    mult = jnp.sqrt(-jnp.expm1(2.0 * log_a))
    u = mult * i * x.astype(jnp.float32)

    def combine(left, right):
        a_l, u_l = left
        a_r, u_r = right
        return a_l * a_r, a_r * u_l + u_r

    _, h = lax.associative_scan(combine, (a, u), axis=1)
    return h.astype(x.dtype)


def token_mixer(h, layer, w_in, b_gate, da_lambda, da_subln_g, conv_w, conv_b,
                lru_wr, lru_br, lru_wi, lru_bi, lru_lambda, w_attn_out, w_lru_out, w_o):
    B, T, _ = h.shape
    z = h @ w_in
    q, k, v, xr, gr, ga, gl = jnp.split(z, PROJ_SPLITS, axis=-1)
    q = q.reshape(B, T, DA_HEADS, 2, DA_HEAD_DIM)
    k = k.reshape(B, T, DA_HEADS, 2, DA_HEAD_DIM)
    v = v.reshape(B, T, DA_HEADS, 2 * DA_HEAD_DIM)
    lam_init = 0.8 - 0.6 * math.exp(-0.3 * layer)
    lf = da_lambda.astype(jnp.float32)
    lam = jnp.exp(jnp.sum(lf[0] * lf[1])) - jnp.exp(jnp.sum(lf[2] * lf[3])) + lam_init
    attn_up = diff_attention(q, k, v, lam, lam_init, da_subln_g) @ w_attn_out
    xr = causal_depthwise_conv(xr, conv_w, conv_b)
    rec = rg_lru(xr, lru_wr, lru_br, lru_wi, lru_bi, lru_lambda) * jax.nn.gelu(gr)
    lru_up = rec @ w_lru_out
    merged = jax.nn.sigmoid(ga + b_gate[0]) * attn_up + jax.nn.sigmoid(gl + b_gate[1]) * lru_up
    return merged @ w_o


def swiglu(h, w_gate, w_up, w_down):
    return (jax.nn.silu(h @ w_gate) * (h @ w_up)) @ w_down


def moe_swiglu(h, w_router, w_gate, w_up, w_down):
    logits = (h @ w_router).astype(jnp.float32)
    top_vals, top_idx = lax.top_k(logits, TOP_K)
    top_w = jax.nn.softmax(top_vals, axis=-1)
    comb = jnp.sum(jax.nn.one_hot(top_idx, N_EXPERTS, dtype=jnp.float32) * top_w[..., None], axis=-2)
    out = jnp.zeros_like(h)
    for e in range(N_EXPERTS):
        out = out + comb[..., e:e + 1].astype(h.dtype) * swiglu(h, w_gate[e], w_up[e], w_down[e])
    return out


def setup_inputs(seed: int = 0) -> dict:
    key = jax.random.key(seed)
    ks = jax.random.split(key, 32)
    f32 = jnp.float32
    nrm = lambda k, shape, s: jax.random.normal(k, shape, f32) * s
    u = jax.random.uniform(ks[13], (DEPTH, LRU_WIDTH), f32, 0.9, 0.999)
    sig = u ** (1.0 / LRU_C)
    lru_lambda = jnp.log(sig) - jnp.log1p(-sig)
    return {
        'x': nrm(ks[0], (BATCH, SEQ, D_MODEL), 1.0),
        'meta_tokens': nrm(ks[1], (N_META, D_MODEL), 1.0),
        'ln_in_g': 1.0 + nrm(ks[2], (D_MODEL,), 0.02),
        'ln_in_b': nrm(ks[3], (D_MODEL,), 0.02),
        'w_in': nrm(ks[4], (DEPTH, D_MODEL, PROJ_WIDTH), D_MODEL ** -0.5),
        'b_gate': nrm(ks[5], (DEPTH, 2, D_MODEL), 0.02),
        'da_lambda': nrm(ks[6], (DEPTH, 4, DA_HEAD_DIM), 0.1),
        'da_subln_g': 1.0 + nrm(ks[7], (DEPTH, 2 * DA_HEAD_DIM), 0.02),
        'conv_w': nrm(ks[8], (DEPTH, CONV_WIDTH, LRU_WIDTH), CONV_WIDTH ** -0.5),
        'conv_b': nrm(ks[9], (DEPTH, LRU_WIDTH), 0.02),
        'lru_wr': nrm(ks[10], (DEPTH, LRU_HEADS, LRU_BLOCK, LRU_BLOCK), LRU_BLOCK ** -0.5),
        'lru_br': nrm(ks[11], (DEPTH, LRU_WIDTH), 0.02),
        'lru_wi': nrm(ks[12], (DEPTH, LRU_HEADS, LRU_BLOCK, LRU_BLOCK), LRU_BLOCK ** -0.5),
        'lru_bi': nrm(ks[14], (DEPTH, LRU_WIDTH), 0.02),
        'lru_lambda': lru_lambda,
        'w_attn_out': nrm(ks[15], (DEPTH, DA_WIDTH, D_MODEL), DA_WIDTH ** -0.5 * BETA),
        'w_lru_out': nrm(ks[16], (DEPTH, LRU_WIDTH, D_MODEL), LRU_WIDTH ** -0.5 * BETA),
        'w_o': nrm(ks[17], (DEPTH, D_MODEL, D_MODEL), D_MODEL ** -0.5 * BETA),
        'ln_g': 1.0 + nrm(ks[18], (DEPTH, 2, D_MODEL), 0.02),
        'ln_b': nrm(ks[19], (DEPTH, 2, D_MODEL), 0.02),
        'ffn_wg': nrm(ks[20], (N_DENSE, D_MODEL, D_FF), D_MODEL ** -0.5),
        'ffn_wu': nrm(ks[21], (N_DENSE, D_MODEL, D_FF), D_MODEL ** -0.5),
        'ffn_wd': nrm(ks[22], (N_DENSE, D_FF, D_MODEL), D_FF ** -0.5 * BETA),
        'router_w': nrm(ks[23], (N_MOE, D_MODEL, N_EXPERTS), D_MODEL ** -0.5),
        'moe_wg': nrm(ks[24], (N_MOE, N_EXPERTS, D_MODEL, D_FF), D_MODEL ** -0.5),
        'moe_wu': nrm(ks[25], (N_MOE, N_EXPERTS, D_MODEL, D_FF), D_MODEL ** -0.5),
        'moe_wd': nrm(ks[26], (N_MOE, N_EXPERTS, D_FF, D_MODEL), D_FF ** -0.5 * BETA),
    }


def reference(x, meta_tokens, ln_in_g, ln_in_b, w_in, b_gate, da_lambda, da_subln_g,
              conv_w, conv_b, lru_wr, lru_br, lru_wi, lru_bi, lru_lambda,
              w_attn_out, w_lru_out, w_o, ln_g, ln_b, ffn_wg, ffn_wu, ffn_wd,
              router_w, moe_wg, moe_wu, moe_wd):
    B = x.shape[0]
    meta = jnp.broadcast_to(meta_tokens[None].astype(x.dtype), (B, N_META, D_MODEL))
    h = layer_norm(jnp.concatenate([meta, x], axis=1), ln_in_g, ln_in_b)
    for i in range(DEPTH):
        mix = token_mixer(h, i, w_in[i], b_gate[i], da_lambda[i], da_subln_g[i],
                          conv_w[i], conv_b[i], lru_wr[i], lru_br[i], lru_wi[i], lru_bi[i],
                          lru_lambda[i], w_attn_out[i], w_lru_out[i], w_o[i])
        h = layer_norm(ALPHA * h + mix, ln_g[i, 0], ln_b[i, 0])
        if i % 2 == 0:
            f = swiglu(h, ffn_wg[i // 2], ffn_wu[i // 2], ffn_wd[i // 2])
        else:
            f = moe_swiglu(h, router_w[i // 2], moe_wg[i // 2], moe_wu[i // 2], moe_wd[i // 2])
        h = layer_norm(ALPHA * h + f, ln_g[i, 1], ln_b[i, 1])
    return h[:, N_META:]
```

```python
import functools
import math

import jax
import jax.numpy as jnp
from jax import lax
from jax.experimental import pallas as pl
from jax.experimental.pallas import tpu as pltpu

D_MODEL = 1024
DEPTH = 4
CHUNK = 64
N_META = 16
DA_HEADS = 8
DA_HEAD_DIM = 32
DA_WIDTH = 512
LRU_WIDTH = 1024
LRU_HEADS = 16
LRU_BLOCK = 64
CONV_WIDTH = 4
LRU_C = 8.0
D_FF = 2816
N_EXPERTS = 8
ALPHA = (2 * DEPTH) ** 0.25
LN_EPS = 1e-5
RMS_EPS = 1e-5
PROJ_WIDTH = 3 * DA_WIDTH + 2 * LRU_WIDTH + 2 * D_MODEL

LANES = 128
META_BLOCK = 128
META_ROW0 = META_BLOCK - N_META
TIME_BLOCK = 128
KV_STEP = 256
LRU_SEG = TIME_BLOCK // 8
FF_CHUNK = 256
N_FF_CHUNKS = D_FF // FF_CHUNK
ROW_TILE = 512
FFN_TILE = 1024
ROUTE_TILE = 512
MOVE_TILE = 256
NEG_BIG = -1e30
VMEM_LIMIT = 56 * 1024 * 1024

BF16 = jnp.bfloat16
F32 = jnp.float32


def _cparams(*sem):
    return pltpu.CompilerParams(dimension_semantics=sem, vmem_limit_bytes=VMEM_LIMIT)


def _layer_norm(x, g, b):
    mu = jnp.mean(x, axis=-1, keepdims=True)
    xc = x - mu
    var = jnp.mean(xc * xc, axis=-1, keepdims=True)
    return xc * lax.rsqrt(var + LN_EPS) * g + b


def _const_spec(shape):
    nd = len(shape)
    return pl.BlockSpec(shape, lambda *_: (0,) * nd)


def _ln_in_kernel(x_ref, meta_ref, g_ref, b_ref, o_ref):
    j = pl.program_id(1)
    last = pl.num_programs(1) - 1

    @pl.when(j < last)
    def _():
        o_ref[0] = _layer_norm(x_ref[0], g_ref[...], b_ref[...])

    @pl.when(j == last)
    def _():
        y = _layer_norm(meta_ref[...], g_ref[...], b_ref[...])
        row = lax.broadcasted_iota(jnp.int32, y.shape, 0)
        o_ref[0] = jnp.where(row >= META_ROW0, y, 0.0)


def _ln_in(x, meta_pad, g, b):
    B, S, D = x.shape
    nblk = S // TIME_BLOCK + 1
    return pl.pallas_call(
        _ln_in_kernel,
        grid=(B, nblk),
        in_specs=[
            pl.BlockSpec((1, TIME_BLOCK, D), lambda b_, j: (b_, jnp.minimum(j, nblk - 2), 0)),
            _const_spec((META_BLOCK, D)),
            _const_spec((1, D)),
            _const_spec((1, D)),
        ],
        out_specs=pl.BlockSpec((1, TIME_BLOCK, D), lambda b_, j: (b_, j, 0)),
        out_shape=jax.ShapeDtypeStruct((B, S + META_BLOCK, D), F32),
        compiler_params=_cparams("parallel", "arbitrary"),
        name="ln_in",
    )(x, meta_pad, g, b)


_PROJ_SEGS = (DA_WIDTH, DA_WIDTH, DA_WIDTH, LRU_WIDTH, LRU_WIDTH, D_MODEL, D_MODEL)


def _in_proj_kernel(h_ref, w_ref, q_ref, k_ref, v_ref, xr_ref, gr_ref, ga_ref, gl_ref):
    hb = h_ref[...].astype(BF16)
    outs = (q_ref, k_ref, v_ref, xr_ref, gr_ref, ga_ref, gl_ref)
    col = 0
    for idx, (o_ref, width) in enumerate(zip(outs, _PROJ_SEGS)):
        for c in range(0, width, 512):
            z = jnp.dot(hb, w_ref[:, col + c:col + c + 512], preferred_element_type=F32)
            if idx == 0:
                z = z * (DA_HEAD_DIM ** -0.5)
            o_ref[:, c:c + 512] = z.astype(BF16)
        col += width


def _in_proj(h2, w_in):
    n, D = h2.shape
    tm = ROW_TILE
    row = lambda w: pl.BlockSpec((tm, w), lambda i: (i, 0))
    return pl.pallas_call(
        _in_proj_kernel,
        grid=(n // tm,),
        in_specs=[row(D), _const_spec((D, PROJ_WIDTH))],
        out_specs=[row(w) for w in _PROJ_SEGS],
        out_shape=[jax.ShapeDtypeStruct((n, w), BF16) for w in _PROJ_SEGS],
        compiler_params=_cparams("parallel"),
        name="in_proj",
    )(h2, w_in)


def _attn_kernel(lam_ref, g_ref, q_ref, k_ref, v_ref, o_ref, qs_scr, m_scr, acc_scr, *, seq, lam_init):
    i = pl.program_id(2)
    n_real = seq // TIME_BLOCK
    tq = TIME_BLOCK
    nt = (((1,), (1,)), ((), ()))

    q = q_ref[0]
    lane = lax.broadcasted_iota(jnp.int32, (tq, LANES), 1)
    for c in range(4):
        qs_scr[c * tq:(c + 1) * tq, :] = jnp.where(lane // DA_HEAD_DIM == c, q, jnp.zeros_like(q))
    qs = qs_scr[...]

    k0 = k_ref[0, seq:seq + META_BLOCK, :]
    v0 = v_ref[0, seq:seq + META_BLOCK, :]
    s = lax.dot_general(qs, k0, nt, preferred_element_type=F32)
    col = lax.broadcasted_iota(jnp.int32, s.shape, 1)
    s = jnp.where(col >= META_ROW0, s, NEG_BIG)
    m0 = jnp.max(s, axis=1, keepdims=True)
    p = jnp.exp(s - m0).astype(BF16)
    vext = jnp.concatenate([v0, jnp.ones_like(v0)], axis=1)
    acc_scr[...] = jnp.dot(p, vext, preferred_element_type=F32)
    m_scr[...] = jnp.broadcast_to(m0, m_scr.shape)

    def step(start, mask):
        ks = k_ref[0, pl.ds(start, KV_STEP), :]
        vs = v_ref[0, pl.ds(start, KV_STEP), :]
        s = lax.dot_general(qs, ks, nt, preferred_element_type=F32)
        if mask is not None:
            s = jnp.where(mask, s, NEG_BIG)
        m_prev = m_scr[...]
        m_new = jnp.maximum(m_prev, jnp.max(s, axis=1, keepdims=True))
        alpha = jnp.exp(m_prev - m_new)
        p = jnp.exp(s - jnp.concatenate([m_new, m_new], axis=1)).astype(BF16)
        vext = jnp.concatenate([vs, jnp.ones_like(vs)], axis=1)
        pv = jnp.dot(p, vext, preferred_element_type=F32)
        acc_scr[...] = acc_scr[...] * jnp.concatenate([alpha, alpha], axis=1) + pv
        m_scr[...] = m_new

    @pl.when(i < n_real)
    def _():
        n_full = (i * tq) // KV_STEP

        def body(j, carry):
            step(pl.multiple_of(j * KV_STEP, KV_STEP), None)
            return carry

        lax.fori_loop(0, n_full, body, 0)

        start = jnp.maximum((i + 1) * tq - KV_STEP, 0)
        shape = (4 * tq, KV_STEP)
        kpos = start + lax.broadcasted_iota(jnp.int32, shape, 1)
        qpos = i * tq + lax.broadcasted_iota(jnp.int32, shape, 0) % tq
        mask = (kpos >= n_full * KV_STEP) & (kpos // CHUNK <= qpos // CHUNK)
        step(pl.multiple_of(start, TIME_BLOCK), mask)

    lf = lam_ref[...]
    lam = (jnp.exp(jnp.sum(lf[0:1] * lf[1:2], axis=1, keepdims=True))
           - jnp.exp(jnp.sum(lf[2:3] * lf[3:4], axis=1, keepdims=True)) + lam_init)
    acc = acc_scr[...]
    o = [acc[c * tq:(c + 1) * tq, :LANES] / acc[c * tq:(c + 1) * tq, LANES:] for c in range(4)]
    low = lane < 2 * DA_HEAD_DIM
    d = jnp.where(low, o[0] - lam * o[1], o[2] - lam * o[3])
    dd = d * d
    ss_lo = jnp.sum(jnp.where(low, dd, 0.0), axis=1, keepdims=True)
    ss_hi = jnp.sum(jnp.where(low, 0.0, dd), axis=1, keepdims=True)
    ms = jnp.where(low, ss_lo, ss_hi) * (1.0 / (2 * DA_HEAD_DIM))
    out = d * lax.rsqrt(ms + RMS_EPS) * g_ref[...] * (1.0 - lam_init)
    o_ref[0] = out.astype(o_ref.dtype)


def _attention(q, k, v, da_lambda, subln_g2, layer, seq):
    B, tp, _ = q.shape
    nblk = tp // TIME_BLOCK
    lam_init = 0.8 - 0.6 * math.exp(-0.3 * layer)
    kern = functools.partial(_attn_kernel, seq=seq, lam_init=lam_init)
    kv_spec = pl.BlockSpec((1, tp, LANES), lambda b_, g, i: (b_, 0, g))
    qo_spec = pl.BlockSpec((1, TIME_BLOCK, LANES), lambda b_, g, i: (b_, i, g))
    return pl.pallas_call(
        kern,
        grid=(B, DA_WIDTH // LANES, nblk),
        in_specs=[_const_spec((4, DA_HEAD_DIM)), _const_spec((1, LANES)), qo_spec, kv_spec, kv_spec],
        out_specs=qo_spec,
        out_shape=jax.ShapeDtypeStruct((B, tp, DA_WIDTH), BF16),
        scratch_shapes=[
            pltpu.VMEM((4 * TIME_BLOCK, LANES), BF16),
            pltpu.VMEM((4 * TIME_BLOCK, LANES), F32),
            pltpu.VMEM((4 * TIME_BLOCK, 2 * LANES), F32),
        ],
        compiler_params=_cparams("parallel", "parallel", "arbitrary"),
        name="diff_attn",
    )(da_lambda, subln_g2, q, k, v)


def _lru_kernel(xr_ref, gr_ref, cw_ref, cb_ref, wg_ref, br_ref, bi_ref, lam_ref, o_ref,
                xbuf, a_buf, u_buf, h_buf, p_buf, carry):
    j = pl.program_id(1)
    tt = TIME_BLOCK

    @pl.when(j == 0)
    def _():
        xbuf[0:8, :] = jnp.zeros((8, LRU_WIDTH), F32)
        carry[...] = jnp.zeros_like(carry)

    live = (lax.broadcasted_iota(jnp.int32, (tt, LANES), 0) >= META_ROW0) | (j > 0)
    for g in range(LRU_WIDTH // LANES):
        sl = slice(g * LANES, (g + 1) * LANES)
        xbuf[8:8 + tt, sl] = jnp.where(live, xr_ref[0, :, sl].astype(F32), 0.0)
    for g in range(LRU_WIDTH // LANES):
        sl = slice(g * LANES, (g + 1) * LANES)
        xc = cb_ref[:, sl] + xbuf[5:5 + tt, sl] * cw_ref[0:1, sl]
        for t in range(1, CONV_WIDTH):
            xc = xc + xbuf[5 + t:5 + t + tt, sl] * cw_ref[t:t + 1, sl]
        lam = lam_ref[:, sl]
        softplus_neg = jnp.maximum(-lam, 0.0) + jnp.log1p(jnp.exp(-jnp.abs(lam)))
        pre = jnp.dot(xc.astype(BF16), wg_ref[g], preferred_element_type=F32)
        r = jax.nn.sigmoid(pre[:, :LANES] + br_ref[:, sl])
        ig = jax.nn.sigmoid(pre[:, LANES:] + bi_ref[:, sl])
        log_a = -LRU_C * r * softplus_neg
        a_buf[g] = jnp.exp(log_a)
        th = jnp.tanh(log_a)
        mult = jnp.sqrt(-2.0 * th / (1.0 - th))
        u_buf[g] = jnp.where(live, mult * ig * xc, 0.0)
    xbuf[0:8, :] = xbuf[tt:tt + 8, :]

    for g in range(LRU_WIDTH // LANES):
        sl = slice(g * LANES, (g + 1) * LANES)
        h = jnp.zeros((8, LANES), F32)
        p = jnp.ones((8, LANES), F32)
        for t in range(LRU_SEG):
            idx = pl.ds(t, 8, stride=LRU_SEG)
            a = a_buf[g, idx, :]
            h = a * h + u_buf[g, idx, :]
            p = a * p
            h_buf[g, idx, :] = h
            p_buf[g, idx, :] = p
        c = carry[0:1, sl]
        inits = []
        for s in range(8):
            inits.append(jnp.broadcast_to(c, (LRU_SEG, LANES)))
            c = h[s:s + 1, :] + p[s:s + 1, :] * c
        carry[:, sl] = jnp.broadcast_to(c, (8, LANES))
        h_true = h_buf[g] + p_buf[g] * jnp.concatenate(inits, axis=0)
        o_ref[0, :, sl] = (h_true * jax.nn.gelu(gr_ref[0, :, sl].astype(F32))).astype(o_ref.dtype)


def _conv_lru(xr, gr, conv_w, conv_b, w_gate, b_r, b_i, lru_lambda):
    B, tp, C = xr.shape
    nblk = tp // TIME_BLOCK
    blk = pl.BlockSpec((1, TIME_BLOCK, C), lambda b_, j: (b_, (j + nblk - 1) % nblk, 0))
    vm = lambda r: pltpu.VMEM((r, C), F32)
    grp = pltpu.VMEM((C // LANES, TIME_BLOCK, LANES), F32)
    return pl.pallas_call(
        _lru_kernel,
        grid=(B, nblk),
        in_specs=[blk, blk, _const_spec((CONV_WIDTH, C)), _const_spec((1, C)),
                  _const_spec((C // LANES, LANES, 2 * LANES)), _const_spec((1, C)), _const_spec((1, C)),
                  _const_spec((1, C))],
        out_specs=blk,
        out_shape=jax.ShapeDtypeStruct((B, tp, C), BF16),
        scratch_shapes=[vm(8 + TIME_BLOCK), grp, grp, grp, grp, vm(8)],
        compiler_params=_cparams("parallel", "arbitrary"),
        name="conv_rglru",
    )(xr, gr, conv_w, conv_b, w_gate, b_r, b_i, lru_lambda)


def _mix_out_kernel(ao_ref, rec_ref, ga_ref, gl_ref, h_ref, wa_ref, wl_ref, wo_ref, bg_ref, g_ref, b_ref, o_ref):
    attn_up = jnp.dot(ao_ref[...], wa_ref[...], preferred_element_type=F32)
    lru_up = jnp.dot(rec_ref[...], wl_ref[...], preferred_element_type=F32)
    merged = (jax.nn.sigmoid(ga_ref[...].astype(F32) + bg_ref[0:1, :]) * attn_up
              + jax.nn.sigmoid(gl_ref[...].astype(F32) + bg_ref[1:2, :]) * lru_up)
    mix = jnp.dot(merged.astype(BF16), wo_ref[...], preferred_element_type=F32)
    o_ref[...] = _layer_norm(ALPHA * h_ref[...] + mix, g_ref[...], b_ref[...])


def _mix_out(ao, rec, ga, gl, h2, wa, wl, wo, bg, g, b):
    n, D = h2.shape
    tm = ROW_TILE
    row = lambda w: pl.BlockSpec((tm, w), lambda i: (i, 0))
    return pl.pallas_call(
        _mix_out_kernel,
        grid=(n // tm,),
        in_specs=[row(DA_WIDTH), row(D), row(D), row(D), row(D),
                  _const_spec((DA_WIDTH, D)), _const_spec((D, D)), _const_spec((D, D)),
                  _const_spec((2, D)), _const_spec((1, D)), _const_spec((1, D))],
        out_specs=row(D),
        out_shape=jax.ShapeDtypeStruct((n, D), F32),
        compiler_params=_cparams("parallel"),
        name="mix_out",
    )(ao, rec, ga, gl, h2, wa, wl, wo, bg, g, b)


def _swiglu_step(xb, wgu_ref, wd_ref, acc_ref, j):
    gu = jnp.dot(xb, wgu_ref[...], preferred_element_type=F32)
    act = (jax.nn.silu(gu[:, :FF_CHUNK]) * gu[:, FF_CHUNK:]).astype(BF16)
    part = jnp.dot(act, wd_ref[...], preferred_element_type=F32)

    @pl.when(j == 0)
    def _():
        acc_ref[...] = part

    @pl.when(j > 0)
    def _():
        acc_ref[...] += part


def _ffn_dense_kernel(x_ref, wgu_ref, wd_ref, g_ref, b_ref, o_ref, xb_scr, acc_scr):
    j = pl.program_id(1)

    @pl.when(j == 0)
    def _():
        xb_scr[...] = x_ref[...].astype(BF16)

    _swiglu_step(xb_scr[...], wgu_ref.at[0], wd_ref.at[0], acc_scr, j)

    @pl.when(j == pl.num_programs(1) - 1)
    def _():
        o_ref[...] = _layer_norm(ALPHA * x_ref[...] + acc_scr[...], g_ref[...], b_ref[...])


def _ffn_dense(h2, wgu, wd, g, b):
    n, D = h2.shape
    tm = FFN_TILE
    return pl.pallas_call(
        _ffn_dense_kernel,
        grid=(n // tm, N_FF_CHUNKS),
        in_specs=[pl.BlockSpec((tm, D), lambda i, j: (i, 0)),
                  pl.BlockSpec((1, D, 2 * FF_CHUNK), lambda i, j: (j, 0, 0)),
                  pl.BlockSpec((1, FF_CHUNK, D), lambda i, j: (j, 0, 0)),
                  _const_spec((1, D)), _const_spec((1, D))],
        out_specs=pl.BlockSpec((tm, D), lambda i, j: (i, 0)),
        out_shape=jax.ShapeDtypeStruct((n, D), F32),
        scratch_shapes=[pltpu.VMEM((tm, D), BF16), pltpu.VMEM((tm, D), F32)],
        compiler_params=_cparams("parallel", "arbitrary"),
        name="ffn_dense",
    )(h2, wgu, wd, g, b)


def _ffn_group_kernel(te_ref, nu_ref, x_ref, wgu_ref, wd_ref, o_ref, xb_scr, acc_scr):
    t = pl.program_id(0)
    j = pl.program_id(1)
    last = pl.num_programs(1) - 1
    used = t < nu_ref[0]

    @pl.when(used & (j == 0))
    def _():
        xb_scr[...] = x_ref[...].astype(BF16)

    @pl.when(used)
    def _():
        _swiglu_step(xb_scr[...], wgu_ref.at[0, 0], wd_ref.at[0, 0], acc_scr, j)

    @pl.when(used & (j == last))
    def _():
        o_ref[...] = acc_scr[...]

    @pl.when(jnp.logical_not(used) & (j == last))
    def _():
        o_ref[...] = jnp.zeros_like(o_ref)


def _ffn_grouped(tile_expert, n_used, xs, wgu, wd):
    n, D = xs.shape
    tm = FFN_TILE
    nf = N_FF_CHUNKS

    def chunk(t, j, te, nu):
        return jnp.where(t < nu[0], j, nf - 1)

    grid_spec = pltpu.PrefetchScalarGridSpec(
        num_scalar_prefetch=2,
        grid=(n // tm, nf),
        in_specs=[pl.BlockSpec((tm, D), lambda t, j, te, nu: (jnp.minimum(t, nu[0] - 1), 0)),
                  pl.BlockSpec((1, 1, D, 2 * FF_CHUNK), lambda t, j, te, nu: (te[t], chunk(t, j, te, nu), 0, 0)),
                  pl.BlockSpec((1, 1, FF_CHUNK, D), lambda t, j, te, nu: (te[t], chunk(t, j, te, nu), 0, 0))],
        out_specs=pl.BlockSpec((tm, D), lambda t, j, te, nu: (t, 0)),
        scratch_shapes=[pltpu.VMEM((tm, D), BF16), pltpu.VMEM((tm, D), F32)],
    )
    return pl.pallas_call(
        _ffn_group_kernel,
        grid_spec=grid_spec,
        out_shape=jax.ShapeDtypeStruct((n, D), F32),
        compiler_params=_cparams("arbitrary", "arbitrary"),
        name="ffn_grouped",
    )(tile_expert, n_used, xs, wgu, wd)


def _router_kernel(h_ref, w_ref, idx_ref, wt_ref):
    nt = (((1,), (1,)), ((), ()))
    h = h_ref[...]
    w = w_ref[...]
    h_hi = h.astype(BF16)
    h_lo = (h - h_hi.astype(F32)).astype(BF16)
    w_hi = w.astype(BF16)
    w_lo = (w - w_hi.astype(F32)).astype(BF16)
    logits = (lax.dot_general(w_hi, h_hi, nt, preferred_element_type=F32)
              + lax.dot_general(w_lo, h_hi, nt, preferred_element_type=F32)
              + lax.dot_general(w_hi, h_lo, nt, preferred_element_type=F32))
    e = lax.broadcasted_iota(jnp.int32, logits.shape, 0)
    v1 = jnp.max(logits, axis=0, keepdims=True)
    i1 = jnp.min(jnp.where(logits == v1, e, N_EXPERTS), axis=0, keepdims=True)
    rest = jnp.where(e == i1, -jnp.inf, logits)
    v2 = jnp.max(rest, axis=0, keepdims=True)
    i2 = jnp.min(jnp.where(rest == v2, e, N_EXPERTS), axis=0, keepdims=True)
    ex = jnp.exp(v2 - v1)
    w1 = 1.0 / (1.0 + ex)
    idx_ref[...] = jnp.concatenate([i1, i2], axis=0)
    wt_ref[...] = jnp.concatenate([w1, ex * w1], axis=0)


def _router(h2, w_t):
    n, D = h2.shape
    tm = ROUTE_TILE
    out = pl.BlockSpec((2, tm), lambda i: (0, i))
    return pl.pallas_call(
        _router_kernel,
        grid=(n // tm,),
        in_specs=[pl.BlockSpec((tm, D), lambda i: (i, 0)), _const_spec((N_EXPERTS, D))],
        out_specs=[out, out],
        out_shape=[jax.ShapeDtypeStruct((2, n), jnp.int32), jax.ShapeDtypeStruct((2, n), F32)],
        compiler_params=_cparams("parallel"),
        name="router_top2",
    )(h2, w_t)


def _rank_kernel(idx_ref, rk_ref, cnt_ref, carry):
    i = pl.program_id(0)
    tm = ROUTE_TILE

    @pl.when(i == 0)
    def _():
        carry[...] = jnp.zeros_like(carry)

    idx = idx_ref[...]
    e = lax.broadcasted_iota(jnp.int32, (N_EXPERTS, tm), 0)
    hit0 = e == idx[0:1, :]
    hit1 = e == idx[1:2, :]
    member = (hit0 | hit1).astype(F32)
    before = (lax.broadcasted_iota(jnp.int32, (tm, tm), 0)
              < lax.broadcasted_iota(jnp.int32, (tm, tm), 1)).astype(BF16)
    rank = jnp.dot(member.astype(BF16), before, preferred_element_type=F32) + carry[:, 0:1]
    r0 = jnp.sum(jnp.where(hit0, rank, 0.0), axis=0, keepdims=True)
    r1 = jnp.sum(jnp.where(hit1, rank, 0.0), axis=0, keepdims=True)
    rk_ref[...] = jnp.concatenate([r0, r1], axis=0).astype(jnp.int32)
    total = carry[...] + jnp.sum(member, axis=1, keepdims=True)
    carry[...] = total
    cnt_ref[...] = total.astype(jnp.int32)


def _rank(idx):
    n = idx.shape[1]
    tm = ROUTE_TILE
    blk = pl.BlockSpec((2, tm), lambda i: (0, i))
    return pl.pallas_call(
        _rank_kernel,
        grid=(n // tm,),
        in_specs=[blk],
        out_specs=[blk, _const_spec((N_EXPERTS, LANES))],
        out_shape=[jax.ShapeDtypeStruct((2, n), jnp.int32), jax.ShapeDtypeStruct((N_EXPERTS, LANES), jnp.int32)],
        scratch_shapes=[pltpu.VMEM((N_EXPERTS, LANES), F32)],
        compiler_params=_cparams("arbitrary"),
        name="route_rank",
    )(idx)


def _row_copy(src, s, dst, d, sem):
    return pltpu.make_async_copy(src.at[pl.ds(s, 1)], dst.at[pl.ds(d, 1)], sem)


def _dispatch_kernel(pos_ref, h_hbm, xs_in, xs_hbm, sem):
    del xs_in
    base = pl.program_id(0) * MOVE_TILE

    def start(r, carry):
        for c in range(2):
            _row_copy(h_hbm, base + r, xs_hbm, pos_ref[0, c, r], sem).start()
        return carry

    lax.fori_loop(0, MOVE_TILE, start, 0, unroll=8)

    def wait(r, carry):
        for c in range(2):
            _row_copy(h_hbm, base + r, xs_hbm, pos_ref[0, c, r], sem).wait()
        return carry

    lax.fori_loop(0, MOVE_TILE, wait, 0, unroll=8)


def _dispatch(pos3, h2, xs_zero):
    n, D = h2.shape
    any_spec = pl.BlockSpec(memory_space=pl.ANY)
    return pl.pallas_call(
        _dispatch_kernel,
        grid=(n // MOVE_TILE,),
        in_specs=[pl.BlockSpec((1, 2, MOVE_TILE), lambda i: (i, 0, 0), memory_space=pltpu.SMEM),
                  any_spec, any_spec],
        out_specs=any_spec,
        out_shape=jax.ShapeDtypeStruct(xs_zero.shape, xs_zero.dtype),
        scratch_shapes=[pltpu.SemaphoreType.DMA(())],
        input_output_aliases={2: 0},
        compiler_params=_cparams("arbitrary"),
        name="moe_dispatch",
    )(pos3, h2, xs_zero)


def _combine_kernel(pos_ref, y_hbm, h_ref, wt_ref, g_ref, b_ref, o_ref, ybuf, sem):
    def start(r, carry):
        for c in range(2):
            _row_copy(y_hbm, pos_ref[0, c, r], ybuf.at[c], r, sem).start()
        return carry

    lax.fori_loop(0, MOVE_TILE, start, 0, unroll=8)

    def wait(r, carry):
        for c in range(2):
            _row_copy(y_hbm, pos_ref[0, c, r], ybuf.at[c], r, sem).wait()
        return carry

    lax.fori_loop(0, MOVE_TILE, wait, 0, unroll=8)

    wt = wt_ref[...]
    f = wt[:, 0:1] * ybuf[0] + wt[:, 1:2] * ybuf[1]
    o_ref[...] = _layer_norm(ALPHA * h_ref[...] + f, g_ref[...], b_ref[...])


def _combine(pos3, y, h2, wt_col, g, b):
    n, D = h2.shape
    tm = MOVE_TILE
    return pl.pallas_call(
        _combine_kernel,
        grid=(n // tm,),
        in_specs=[pl.BlockSpec((1, 2, tm), lambda i: (i, 0, 0), memory_space=pltpu.SMEM),
                  pl.BlockSpec(memory_space=pl.ANY),
                  pl.BlockSpec((tm, D), lambda i: (i, 0)),
                  pl.BlockSpec((tm, 2), lambda i: (i, 0)),
                  _const_spec((1, D)), _const_spec((1, D))],
        out_specs=pl.BlockSpec((tm, D), lambda i: (i, 0)),
        out_shape=jax.ShapeDtypeStruct((n, D), F32),
        scratch_shapes=[pltpu.VMEM((2, tm, D), F32), pltpu.SemaphoreType.DMA(())],
        compiler_params=_cparams("arbitrary"),
        name="moe_combine",
    )(pos3, y, h2, wt_col, g, b)


def _moe(h2, w_router_t, wgu, wd, g, b):
    n, D = h2.shape
    tm = FFN_TILE
    idx, wt = _router(h2, w_router_t)
    rank, cnt = _rank(idx)

    counts = cnt[:, 0]
    padded = (counts + tm - 1) // tm * tm
    ends = jnp.cumsum(padded)
    starts = ends - padded
    pos = starts[idx] + rank
    n_tiles = (2 * n) // tm + N_EXPERTS
    n_used = (ends[-1] // tm).astype(jnp.int32)
    tile_start = jnp.arange(n_tiles, dtype=jnp.int32) * tm
    tile_expert = jnp.sum(tile_start[:, None] >= ends[None, :], axis=1).astype(jnp.int32)
    tile_expert = jnp.minimum(tile_expert, tile_expert[jnp.maximum(n_used - 1, 0)])
    pos3 = pos.reshape(2, n // MOVE_TILE, MOVE_TILE).transpose(1, 0, 2)

    xs = _dispatch(pos3, h2, jnp.zeros((n_tiles * tm, D), F32))
    y = _ffn_grouped(tile_expert, n_used.reshape(1), xs, wgu, wd)
    return _combine(pos3, y, h2, wt.T, g, b)


def _pack_ffn(wg, wu, wd):
    lead = wg.shape[:-2]
    split = lambda w: w.astype(BF16).reshape(*lead, D_MODEL, N_FF_CHUNKS, FF_CHUNK)
    wgu = jnp.concatenate([split(wg), split(wu)], axis=-1)
    wgu = jnp.moveaxis(wgu, -2, -3)
    return wgu, wd.astype(BF16).reshape(*lead, N_FF_CHUNKS, FF_CHUNK, D_MODEL)


def _pack_lru_gates(w_r, w_i):
    def pair(w):
        w = w.reshape(LRU_HEADS // 2, 2, LRU_BLOCK, LRU_BLOCK)
        z = jnp.zeros_like(w[:, 0])
        top = jnp.concatenate([w[:, 0], z], axis=-1)
        bot = jnp.concatenate([z, w[:, 1]], axis=-1)
        return jnp.concatenate([top, bot], axis=-2)
    return jnp.concatenate([pair(w_r), pair(w_i)], axis=-1).astype(BF16)


def kernel(x, meta_tokens, ln_in_g, ln_in_b, w_in, b_gate, da_lambda, da_subln_g, conv_w, conv_b, lru_wr, lru_br, lru_wi, lru_bi, lru_lambda, w_attn_out, w_lru_out, w_o, ln_g, ln_b, ffn_wg, ffn_wu, ffn_wd, router_w, moe_wg, moe_wu, moe_wd):
    B, S, D = x.shape
    tp = S + META_BLOCK
    n = B * tp
    row = lambda a: a.reshape(1, -1)

    meta_pad = jnp.pad(meta_tokens, ((META_ROW0, 0), (0, 0)))
    h = _ln_in(x, meta_pad, row(ln_in_g), row(ln_in_b)).reshape(n, D)

    w_in_b = w_in.astype(BF16)
    wa_b, wl_b, wo_b = w_attn_out.astype(BF16), w_lru_out.astype(BF16), w_o.astype(BF16)
    dense_wgu, dense_wd = _pack_ffn(ffn_wg, ffn_wu, ffn_wd)
    moe_wgu, moe_wdn = _pack_ffn(moe_wg, moe_wu, moe_wd)

    for i in range(DEPTH):
        q, k, v, xr, gr, ga, gl = _in_proj(h, w_in_b[i])
        seq3 = lambda a: a.reshape(B, tp, a.shape[-1])
        ao = _attention(seq3(q), seq3(k), seq3(v), da_lambda[i], row(jnp.tile(da_subln_g[i], 2)), i, S)
        rec = _conv_lru(seq3(xr), seq3(gr), conv_w[i], row(conv_b[i]), _pack_lru_gates(lru_wr[i], lru_wi[i]),
                        row(lru_br[i]), row(lru_bi[i]), row(lru_lambda[i]))
        h = _mix_out(ao.reshape(n, DA_WIDTH), rec.reshape(n, D), ga, gl, h, wa_b[i], wl_b[i], wo_b[i],
                     b_gate[i], row(ln_g[i, 0]), row(ln_b[i, 0]))
        if i % 2 == 0:
            h = _ffn_dense(h, dense_wgu[i // 2], dense_wd[i // 2], row(ln_g[i, 1]), row(ln_b[i, 1]))
        else:
            h = _moe(h, router_w[i // 2].T, moe_wgu[i // 2], moe_wdn[i // 2], row(ln_g[i, 1]), row(ln_b[i, 1]))
    return h.reshape(B, tp, D)[:, :S]
```

```python
import functools
import math

import jax
import jax.numpy as jnp
from jax import lax
from jax.experimental import pallas as pl
from jax.experimental.pallas import tpu as pltpu

D_MODEL = 1024
DEPTH = 4
CHUNK = 64
N_META = 16
DA_HEADS = 8
DA_HEAD_DIM = 32
DA_WIDTH = 512
LRU_WIDTH = 1024
LRU_HEADS = 16
LRU_BLOCK = 64
CONV_WIDTH = 4
LRU_C = 8.0
D_FF = 2816
N_EXPERTS = 8
ALPHA = (2 * DEPTH) ** 0.25
LN_EPS = 1e-5
RMS_EPS = 1e-5
PROJ_WIDTH = 3 * DA_WIDTH + 2 * LRU_WIDTH + 2 * D_MODEL

LANES = 128
META_BLOCK = 128
META_ROW0 = META_BLOCK - N_META
TIME_BLOCK = 128
ATTN_Q = 256
LRU_SEG = TIME_BLOCK // 8
FF_CHUNK = 256
N_FF_CHUNKS = D_FF // FF_CHUNK
ROW_TILE = 512
FFN_TILE = 1024
ROUTE_TILE = 512
MOVE_TILE = 256
NEG_BIG = -1e30
VMEM_LIMIT = 56 * 1024 * 1024

BF16 = jnp.bfloat16
F32 = jnp.float32


def _cparams(*sem):
    return pltpu.CompilerParams(dimension_semantics=sem, vmem_limit_bytes=VMEM_LIMIT)


def _layer_norm(x, g, b):
    mu = jnp.mean(x, axis=-1, keepdims=True)
    xc = x - mu
    var = jnp.mean(xc * xc, axis=-1, keepdims=True)
    return xc * lax.rsqrt(var + LN_EPS) * g + b


def _const_spec(shape):
    nd = len(shape)
    return pl.BlockSpec(shape, lambda *_: (0,) * nd)


def _ln_in_kernel(x_ref, meta_ref, g_ref, b_ref, o_ref):
    j = pl.program_id(1)
    last = pl.num_programs(1) - 1

    @pl.when(j < last)
    def _():
        o_ref[0] = _layer_norm(x_ref[0], g_ref[...], b_ref[...])

    @pl.when(j == last)
    def _():
        y = _layer_norm(meta_ref[...], g_ref[...], b_ref[...])
        row = lax.broadcasted_iota(jnp.int32, y.shape, 0)
        o_ref[0] = jnp.where(row >= META_ROW0, y, 0.0)


def _ln_in(x, meta_pad, g, b):
    B, S, D = x.shape
    nblk = S // TIME_BLOCK + 1
    return pl.pallas_call(
        _ln_in_kernel,
        grid=(B, nblk),
        in_specs=[
            pl.BlockSpec((1, TIME_BLOCK, D), lambda b_, j: (b_, jnp.minimum(j, nblk - 2), 0)),
            _const_spec((META_BLOCK, D)),
            _const_spec((1, D)),
            _const_spec((1, D)),
        ],
        out_specs=pl.BlockSpec((1, TIME_BLOCK, D), lambda b_, j: (b_, j, 0)),
        out_shape=jax.ShapeDtypeStruct((B, S + META_BLOCK, D), F32),
        compiler_params=_cparams("parallel", "arbitrary"),
        name="ln_in",
    )(x, meta_pad, g, b)


_PROJ_SEGS = (DA_WIDTH, DA_WIDTH, DA_WIDTH, LRU_WIDTH, LRU_WIDTH, D_MODEL, D_MODEL)


def _in_proj_kernel(h_ref, w_ref, q_ref, k_ref, v_ref, xr_ref, gr_ref, ga_ref, gl_ref):
    hb = h_ref[...].astype(BF16)
    outs = (q_ref, k_ref, v_ref, xr_ref, gr_ref, ga_ref, gl_ref)
    col = 0
    for idx, (o_ref, width) in enumerate(zip(outs, _PROJ_SEGS)):
        for c in range(0, width, 512):
            z = jnp.dot(hb, w_ref[:, col + c:col + c + 512], preferred_element_type=F32)
            if idx == 0:
                z = z * (DA_HEAD_DIM ** -0.5 * math.log2(math.e))
            o_ref[:, c:c + 512] = z.astype(BF16)
        col += width


def _in_proj(h2, w_in):
    n, D = h2.shape
    tm = ROW_TILE
    row = lambda w: pl.BlockSpec((tm, w), lambda i: (i, 0))
    return pl.pallas_call(
        _in_proj_kernel,
        grid=(n // tm,),
        in_specs=[row(D), _const_spec((D, PROJ_WIDTH))],
        out_specs=[row(w) for w in _PROJ_SEGS],
        out_shape=[jax.ShapeDtypeStruct((n, w), BF16) for w in _PROJ_SEGS],
        compiler_params=_cparams("parallel"),
        name="in_proj",
    )(h2, w_in)


_NT = (((1,), (1,)), ((), ()))
HEAD_LANES = 2 * DA_HEAD_DIM


def _attn_stack_queries(q, qs_scr):
    tq = q.shape[0]
    lane = lax.broadcasted_iota(jnp.int32, q.shape, 1)
    zero = jnp.zeros_like(q)
    for x in range(2):
        for m in range(2):
            qs_scr[x, m * tq:(m + 1) * tq, :] = jnp.where(lane // DA_HEAD_DIM == 2 * x + m, q, zero)


def _attn_scores(qs_scr, x, ks):
    return lax.dot_general(qs_scr[x], ks, _NT, preferred_element_type=F32)


def _attn_update(m_scr, acc_scr, scores, vs, mask, first):
    tk = vs.shape[0]
    lane = lax.broadcasted_iota(jnp.int32, vs.shape, 1)
    one = jnp.ones_like(vs)
    rhs = (jnp.where(lane < HEAD_LANES, vs, one), jnp.where(lane < HEAD_LANES, one, vs))
    for x in range(2):
        s = scores[x]
        if mask is not None:
            s = jnp.where(mask, s, NEG_BIG)
        m_cur = jnp.max(s, axis=1, keepdims=True)
        if first:
            m_new = jnp.broadcast_to(m_cur, m_scr.shape[1:])
        else:
            m_prev = m_scr[x]
            m_new = jnp.maximum(m_prev, m_cur)
        p = jnp.exp2(s - jnp.concatenate([m_new] * (tk // LANES), axis=1)).astype(BF16)
        pv = jnp.dot(p, rhs[x], preferred_element_type=F32)
        if first:
            acc_scr[x] = pv
        else:
            acc_scr[x] = acc_scr[x] * jnp.exp2(m_prev - m_new) + pv
        m_scr[x] = m_new


def _attn_finish(lam_ref, g_ref, acc_scr, o_ref, lam_init):
    tq = o_ref.shape[1]
    lf = lam_ref[...]
    lam = (jnp.exp(jnp.sum(lf[0:1] * lf[1:2], axis=1, keepdims=True))
           - jnp.exp(jnp.sum(lf[2:3] * lf[3:4], axis=1, keepdims=True)) + lam_init)
    diff = []
    for x in range(2):
        acc = acc_scr[x]
        o = acc / pltpu.roll(acc, HEAD_LANES, axis=1)
        diff.append(o[:tq] - lam * o[tq:])
    low = lax.broadcasted_iota(jnp.int32, (tq, LANES), 1) < HEAD_LANES
    d = jnp.where(low, diff[0], diff[1])
    dd = d * d
    ss_lo = jnp.sum(jnp.where(low, dd, 0.0), axis=1, keepdims=True)
    ss_hi = jnp.sum(jnp.where(low, 0.0, dd), axis=1, keepdims=True)
    ms = jnp.where(low, ss_lo, ss_hi) * (1.0 / HEAD_LANES)
    out = d * lax.rsqrt(ms + RMS_EPS) * g_ref[...] * (1.0 - lam_init)
    o_ref[0] = out.astype(o_ref.dtype)


def _meta_key_mask(rows):
    return lax.broadcasted_iota(jnp.int32, (rows, META_BLOCK), 1) >= META_ROW0


def _attn_kernel(lam_ref, g_ref, q_ref, k_ref, v_ref, o_in_ref, o_ref, qs_scr, m_scr, acc_scr, sa_scr, sb_scr, *,
                 seq, lam_init):
    del o_in_ref
    i = pl.program_id(2)
    tq = ATTN_Q
    _attn_stack_queries(q_ref[0], qs_scr)
    keys = lambda j: k_ref[0, pl.ds(pl.multiple_of(j * tq, tq), tq), :]
    vals = lambda j: v_ref[0, pl.ds(pl.multiple_of(j * tq, tq), tq), :]

    for x in range(2):
        sa_scr[x] = _attn_scores(qs_scr, x, keys(0))

    k0 = k_ref[0, seq:seq + META_BLOCK, :]
    _attn_update(m_scr, acc_scr, [_attn_scores(qs_scr, x, k0) for x in range(2)],
                 v_ref[0, seq:seq + META_BLOCK, :], _meta_key_mask(2 * tq), first=True)

    def full_block(j, cur, nxt):
        for x in range(2):
            nxt[x] = _attn_scores(qs_scr, x, keys(j + 1))
        _attn_update(m_scr, acc_scr, [cur[x] for x in range(2)], vals(j), None, first=False)

    def diagonal_block(cur):
        row = lax.broadcasted_iota(jnp.int32, (2 * tq, tq), 0) % tq
        col = lax.broadcasted_iota(jnp.int32, (2 * tq, tq), 1)
        _attn_update(m_scr, acc_scr, [cur[x] for x in range(2)], vals(i), col // CHUNK <= row // CHUNK,
                     first=False)

    def body(jj, carry):
        full_block(2 * jj, sa_scr, sb_scr)
        full_block(2 * jj + 1, sb_scr, sa_scr)
        return carry

    lax.fori_loop(0, i // 2, body, 0)

    @pl.when(i % 2 == 1)
    def _():
        full_block(i - 1, sa_scr, sb_scr)
        diagonal_block(sb_scr)

    @pl.when(i % 2 == 0)
    def _():
        diagonal_block(sa_scr)

    _attn_finish(lam_ref, g_ref, acc_scr, o_ref, lam_init)


def _attn_meta_kernel(lam_ref, g_ref, q_ref, k_ref, v_ref, o_in_ref, o_ref, qs_scr, m_scr, acc_scr, *, lam_init):
    del o_in_ref
    _attn_stack_queries(q_ref[0], qs_scr)
    _attn_update(m_scr, acc_scr, [_attn_scores(qs_scr, x, k_ref[0]) for x in range(2)], v_ref[0],
                 _meta_key_mask(2 * META_BLOCK), first=True)
    _attn_finish(lam_ref, g_ref, acc_scr, o_ref, lam_init)


def _attn_scratch(tq):
    return [pltpu.VMEM((2, 2 * tq, LANES), BF16), pltpu.VMEM((2, 2 * tq, LANES), F32),
            pltpu.VMEM((2, 2 * tq, LANES), F32)]


def _attention(q, k, v, da_lambda, subln_g2, layer, seq):
    B, tp, _ = q.shape
    lam_init = 0.8 - 0.6 * math.exp(-0.3 * layer)
    groups = DA_WIDTH // LANES
    small = [_const_spec((4, DA_HEAD_DIM)), _const_spec((1, LANES))]
    kv_spec = pl.BlockSpec((1, tp, LANES), lambda b_, g, i: (b_, 0, g))
    qo_spec = pl.BlockSpec((1, ATTN_Q, LANES), lambda b_, g, i: (b_, i, g))
    o = pl.pallas_call(
        functools.partial(_attn_kernel, seq=seq, lam_init=lam_init),
        grid=(B, groups, seq // ATTN_Q),
        in_specs=small + [qo_spec, kv_spec, kv_spec, pl.BlockSpec(memory_space=pl.ANY)],
        out_specs=qo_spec,
        out_shape=jax.ShapeDtypeStruct((B, tp, DA_WIDTH), BF16),
        scratch_shapes=_attn_scratch(ATTN_Q) + [pltpu.VMEM((2, 2 * ATTN_Q, ATTN_Q), F32)] * 2,
        input_output_aliases={5: 0},
        compiler_params=_cparams("parallel", "parallel", "arbitrary"),
        name="diff_attn",
    )(da_lambda, subln_g2, q, k, v, jnp.zeros((B, tp, DA_WIDTH), BF16))
    meta_spec = pl.BlockSpec((1, META_BLOCK, LANES), lambda b_, g: (b_, seq // META_BLOCK, g))
    return pl.pallas_call(
        functools.partial(_attn_meta_kernel, lam_init=lam_init),
        grid=(B, groups),
        in_specs=small + [meta_spec, meta_spec, meta_spec, pl.BlockSpec(memory_space=pl.ANY)],
        out_specs=meta_spec,
        out_shape=jax.ShapeDtypeStruct((B, tp, DA_WIDTH), BF16),
        scratch_shapes=_attn_scratch(META_BLOCK),
        input_output_aliases={5: 0},
        compiler_params=_cparams("parallel", "parallel"),
        name="diff_attn_meta",
    )(da_lambda, subln_g2, q, k, v, o)


def _lru_kernel(xr_ref, gr_ref, cw_ref, cb_ref, wg_ref, br_ref, bi_ref, lam_ref, o_ref,
                xbuf, a_buf, u_buf, h_buf, p_buf, carry):
    j = pl.program_id(1)
    tt = TIME_BLOCK

    @pl.when(j == 0)
    def _():
        xbuf[0:8, :] = jnp.zeros((8, LRU_WIDTH), F32)
        carry[...] = jnp.zeros_like(carry)

    live = (lax.broadcasted_iota(jnp.int32, (tt, LANES), 0) >= META_ROW0) | (j > 0)
    for g in range(LRU_WIDTH // LANES):
        sl = slice(g * LANES, (g + 1) * LANES)
        xbuf[8:8 + tt, sl] = jnp.where(live, xr_ref[0, :, sl].astype(F32), 0.0)
    for g in range(LRU_WIDTH // LANES):
        sl = slice(g * LANES, (g + 1) * LANES)
        xc = cb_ref[:, sl] + xbuf[5:5 + tt, sl] * cw_ref[0:1, sl]
        for t in range(1, CONV_WIDTH):
            xc = xc + xbuf[5 + t:5 + t + tt, sl] * cw_ref[t:t + 1, sl]
        lam = lam_ref[:, sl]
        softplus_neg = jnp.maximum(-lam, 0.0) + jnp.log1p(jnp.exp(-jnp.abs(lam)))
        pre = jnp.dot(xc.astype(BF16), wg_ref[g], preferred_element_type=F32)
        r = jax.nn.sigmoid(pre[:, :LANES] + br_ref[:, sl])
        ig = jax.nn.sigmoid(pre[:, LANES:] + bi_ref[:, sl])
        log_a = -LRU_C * r * softplus_neg
        a_buf[g] = jnp.exp(log_a)
        th = jnp.tanh(log_a)
        mult = jnp.sqrt(-2.0 * th / (1.0 - th))
        u_buf[g] = jnp.where(live, mult * ig * xc, 0.0)
    xbuf[0:8, :] = xbuf[tt:tt + 8, :]

    for g in range(LRU_WIDTH // LANES):
        sl = slice(g * LANES, (g + 1) * LANES)
        h = jnp.zeros((8, LANES), F32)
        p = jnp.ones((8, LANES), F32)
        for t in range(LRU_SEG):
            idx = pl.ds(t, 8, stride=LRU_SEG)
            a = a_buf[g, idx, :]
            h = a * h + u_buf[g, idx, :]
            p = a * p
            h_buf[g, idx, :] = h
            p_buf[g, idx, :] = p
        c = carry[0:1, sl]
        inits = []
        for s in range(8):
            inits.append(jnp.broadcast_to(c, (LRU_SEG, LANES)))
            c = h[s:s + 1, :] + p[s:s + 1, :] * c
        carry[:, sl] = jnp.broadcast_to(c, (8, LANES))
        h_true = h_buf[g] + p_buf[g] * jnp.concatenate(inits, axis=0)
        o_ref[0, :, sl] = (h_true * jax.nn.gelu(gr_ref[0, :, sl].astype(F32))).astype(o_ref.dtype)


def _conv_lru(xr, gr, conv_w, conv_b, w_gate, b_r, b_i, lru_lambda):
    B, tp, C = xr.shape
    nblk = tp // TIME_BLOCK
    blk = pl.BlockSpec((1, TIME_BLOCK, C), lambda b_, j: (b_, (j + nblk - 1) % nblk, 0))
    vm = lambda r: pltpu.VMEM((r, C), F32)
    grp = pltpu.VMEM((C // LANES, TIME_BLOCK, LANES), F32)
    return pl.pallas_call(
        _lru_kernel,
        grid=(B, nblk),
        in_specs=[blk, blk, _const_spec((CONV_WIDTH, C)), _const_spec((1, C)),
                  _const_spec((C // LANES, LANES, 2 * LANES)), _const_spec((1, C)), _const_spec((1, C)),
                  _const_spec((1, C))],
        out_specs=blk,
        out_shape=jax.ShapeDtypeStruct((B, tp, C), BF16),
        scratch_shapes=[vm(8 + TIME_BLOCK), grp, grp, grp, grp, vm(8)],
        compiler_params=_cparams("parallel", "arbitrary"),
        name="conv_rglru",
    )(xr, gr, conv_w, conv_b, w_gate, b_r, b_i, lru_lambda)


def _mix_out_kernel(ao_ref, rec_ref, ga_ref, gl_ref, h_ref, wa_ref, wl_ref, wo_ref, bg_ref, g_ref, b_ref, o_ref):
    attn_up = jnp.dot(ao_ref[...], wa_ref[...], preferred_element_type=F32)
    lru_up = jnp.dot(rec_ref[...], wl_ref[...], preferred_element_type=F32)
    merged = (jax.nn.sigmoid(ga_ref[...].astype(F32) + bg_ref[0:1, :]) * attn_up
              + jax.nn.sigmoid(gl_ref[...].astype(F32) + bg_ref[1:2, :]) * lru_up)
    mix = jnp.dot(merged.astype(BF16), wo_ref[...], preferred_element_type=F32)
    o_ref[...] = _layer_norm(ALPHA * h_ref[...] + mix, g_ref[...], b_ref[...])


def _mix_out(ao, rec, ga, gl, h2, wa, wl, wo, bg, g, b):
    n, D = h2.shape
    tm = ROW_TILE
    row = lambda w: pl.BlockSpec((tm, w), lambda i: (i, 0))
    return pl.pallas_call(
        _mix_out_kernel,
        grid=(n // tm,),
        in_specs=[row(DA_WIDTH), row(D), row(D), row(D), row(D),
                  _const_spec((DA_WIDTH, D)), _const_spec((D, D)), _const_spec((D, D)),
                  _const_spec((2, D)), _const_spec((1, D)), _const_spec((1, D))],
        out_specs=row(D),
        out_shape=jax.ShapeDtypeStruct((n, D), F32),
        compiler_params=_cparams("parallel"),
        name="mix_out",
    )(ao, rec, ga, gl, h2, wa, wl, wo, bg, g, b)


def _swiglu_step(xb, wgu_ref, wd_ref, acc_ref, j):
    gu = jnp.dot(xb, wgu_ref[...], preferred_element_type=F32)
    act = (jax.nn.silu(gu[:, :FF_CHUNK]) * gu[:, FF_CHUNK:]).astype(BF16)
    part = jnp.dot(act, wd_ref[...], preferred_element_type=F32)

    @pl.when(j == 0)
    def _():
        acc_ref[...] = part

    @pl.when(j > 0)
    def _():
        acc_ref[...] += part


def _ffn_dense_kernel(x_ref, wgu_ref, wd_ref, g_ref, b_ref, o_ref, xb_scr, acc_scr):
    j = pl.program_id(1)

    @pl.when(j == 0)
    def _():
        xb_scr[...] = x_ref[...].astype(BF16)

    _swiglu_step(xb_scr[...], wgu_ref.at[0], wd_ref.at[0], acc_scr, j)

    @pl.when(j == pl.num_programs(1) - 1)
    def _():
        o_ref[...] = _layer_norm(ALPHA * x_ref[...] + acc_scr[...], g_ref[...], b_ref[...])


def _ffn_dense(h2, wgu, wd, g, b):
    n, D = h2.shape
    tm = FFN_TILE
    return pl.pallas_call(
        _ffn_dense_kernel,
        grid=(n // tm, N_FF_CHUNKS),
        in_specs=[pl.BlockSpec((tm, D), lambda i, j: (i, 0)),
                  pl.BlockSpec((1, D, 2 * FF_CHUNK), lambda i, j: (j, 0, 0)),
                  pl.BlockSpec((1, FF_CHUNK, D), lambda i, j: (j, 0, 0)),
                  _const_spec((1, D)), _const_spec((1, D))],
        out_specs=pl.BlockSpec((tm, D), lambda i, j: (i, 0)),
        out_shape=jax.ShapeDtypeStruct((n, D), F32),
        scratch_shapes=[pltpu.VMEM((tm, D), BF16), pltpu.VMEM((tm, D), F32)],
        compiler_params=_cparams("parallel", "arbitrary"),
        name="ffn_dense",
    )(h2, wgu, wd, g, b)


def _ffn_group_kernel(te_ref, nu_ref, x_ref, wgu_ref, wd_ref, o_ref, xb_scr, acc_scr):
    t = pl.program_id(0)
    j = pl.program_id(1)
    last = pl.num_programs(1) - 1
    used = t < nu_ref[0]

    @pl.when(used & (j == 0))
    def _():
        xb_scr[...] = x_ref[...].astype(BF16)

    @pl.when(used)
    def _():
        _swiglu_step(xb_scr[...], wgu_ref.at[0, 0], wd_ref.at[0, 0], acc_scr, j)

    @pl.when(used & (j == last))
    def _():
        o_ref[...] = acc_scr[...]

    @pl.when(jnp.logical_not(used) & (j == last))
    def _():
        o_ref[...] = jnp.zeros_like(o_ref)


def _ffn_grouped(tile_expert, n_used, xs, wgu, wd):
    n, D = xs.shape
    tm = FFN_TILE
    nf = N_FF_CHUNKS

    def chunk(t, j, te, nu):
        return jnp.where(t < nu[0], j, nf - 1)

    grid_spec = pltpu.PrefetchScalarGridSpec(
        num_scalar_prefetch=2,
        grid=(n // tm, nf),
        in_specs=[pl.BlockSpec((tm, D), lambda t, j, te, nu: (jnp.minimum(t, nu[0] - 1), 0)),
                  pl.BlockSpec((1, 1, D, 2 * FF_CHUNK), lambda t, j, te, nu: (te[t], chunk(t, j, te, nu), 0, 0)),
                  pl.BlockSpec((1, 1, FF_CHUNK, D), lambda t, j, te, nu: (te[t], chunk(t, j, te, nu), 0, 0))],
        out_specs=pl.BlockSpec((tm, D), lambda t, j, te, nu: (t, 0)),
        scratch_shapes=[pltpu.VMEM((tm, D), BF16), pltpu.VMEM((tm, D), F32)],
    )
    return pl.pallas_call(
        _ffn_group_kernel,
        grid_spec=grid_spec,
        out_shape=jax.ShapeDtypeStruct((n, D), F32),
        compiler_params=_cparams("arbitrary", "arbitrary"),
        name="ffn_grouped",
    )(tile_expert, n_used, xs, wgu, wd)


def _router_kernel(h_ref, w_ref, idx_ref, wt_ref):
    nt = (((1,), (1,)), ((), ()))
    h = h_ref[...]
    w = w_ref[...]
    h_hi = h.astype(BF16)
    h_lo = (h - h_hi.astype(F32)).astype(BF16)
    w_hi = w.astype(BF16)
    w_lo = (w - w_hi.astype(F32)).astype(BF16)
    logits = (lax.dot_general(w_hi, h_hi, nt, preferred_element_type=F32)
              + lax.dot_general(w_lo, h_hi, nt, preferred_element_type=F32)
              + lax.dot_general(w_hi, h_lo, nt, preferred_element_type=F32))
    e = lax.broadcasted_iota(jnp.int32, logits.shape, 0)
    v1 = jnp.max(logits, axis=0, keepdims=True)
    i1 = jnp.min(jnp.where(logits == v1, e, N_EXPERTS), axis=0, keepdims=True)
    rest = jnp.where(e == i1, -jnp.inf, logits)
    v2 = jnp.max(rest, axis=0, keepdims=True)
    i2 = jnp.min(jnp.where(rest == v2, e, N_EXPERTS), axis=0, keepdims=True)
    ex = jnp.exp(v2 - v1)
    w1 = 1.0 / (1.0 + ex)
    idx_ref[...] = jnp.concatenate([i1, i2], axis=0)
    wt_ref[...] = jnp.concatenate([w1, ex * w1], axis=0)


def _router(h2, w_t):
    n, D = h2.shape
    tm = ROUTE_TILE
    out = pl.BlockSpec((2, tm), lambda i: (0, i))
    return pl.pallas_call(
        _router_kernel,
        grid=(n // tm,),
        in_specs=[pl.BlockSpec((tm, D), lambda i: (i, 0)), _const_spec((N_EXPERTS, D))],
        out_specs=[out, out],
        out_shape=[jax.ShapeDtypeStruct((2, n), jnp.int32), jax.ShapeDtypeStruct((2, n), F32)],
        compiler_params=_cparams("parallel"),
        name="router_top2",
    )(h2, w_t)


def _rank_kernel(idx_ref, rk_ref, cnt_ref, carry):
    i = pl.program_id(0)
    tm = ROUTE_TILE

    @pl.when(i == 0)
    def _():
        carry[...] = jnp.zeros_like(carry)

    idx = idx_ref[...]
    e = lax.broadcasted_iota(jnp.int32, (N_EXPERTS, tm), 0)
    hit0 = e == idx[0:1, :]
    hit1 = e == idx[1:2, :]
    member = (hit0 | hit1).astype(F32)
    before = (lax.broadcasted_iota(jnp.int32, (tm, tm), 0)
              < lax.broadcasted_iota(jnp.int32, (tm, tm), 1)).astype(BF16)
    rank = jnp.dot(member.astype(BF16), before, preferred_element_type=F32) + carry[:, 0:1]
    r0 = jnp.sum(jnp.where(hit0, rank, 0.0), axis=0, keepdims=True)
    r1 = jnp.sum(jnp.where(hit1, rank, 0.0), axis=0, keepdims=True)
    rk_ref[...] = jnp.concatenate([r0, r1], axis=0).astype(jnp.int32)
    total = carry[...] + jnp.sum(member, axis=1, keepdims=True)
    carry[...] = total
    cnt_ref[...] = total.astype(jnp.int32)


def _rank(idx):
    n = idx.shape[1]
    tm = ROUTE_TILE
    blk = pl.BlockSpec((2, tm), lambda i: (0, i))
    return pl.pallas_call(
        _rank_kernel,
        grid=(n // tm,),
        in_specs=[blk],
        out_specs=[blk, _const_spec((N_EXPERTS, LANES))],
        out_shape=[jax.ShapeDtypeStruct((2, n), jnp.int32), jax.ShapeDtypeStruct((N_EXPERTS, LANES), jnp.int32)],
        scratch_shapes=[pltpu.VMEM((N_EXPERTS, LANES), F32)],
        compiler_params=_cparams("arbitrary"),
        name="route_rank",
    )(idx)


def _row_copy(src, s, dst, d, sem):
    return pltpu.make_async_copy(src.at[pl.ds(s, 1)], dst.at[pl.ds(d, 1)], sem)


def _dispatch_kernel(pos_ref, h_ref, xs_in, xs_hbm, sem):
    del xs_in

    def start(r, carry):
        for c in range(2):
            _row_copy(h_ref, r, xs_hbm, pos_ref[0, c, r], sem).start()
        return carry

    lax.fori_loop(0, MOVE_TILE, start, 0, unroll=8)

    def wait(r, carry):
        for c in range(2):
            _row_copy(h_ref, r, xs_hbm, pos_ref[0, c, r], sem).wait()
        return carry

    lax.fori_loop(0, MOVE_TILE, wait, 0, unroll=8)


def _dispatch(pos3, h2, xs_zero):
    n, D = h2.shape
    any_spec = pl.BlockSpec(memory_space=pl.ANY)
    return pl.pallas_call(
        _dispatch_kernel,
        grid=(n // MOVE_TILE,),
        in_specs=[pl.BlockSpec((1, 2, MOVE_TILE), lambda i: (i, 0, 0), memory_space=pltpu.SMEM),
                  pl.BlockSpec((MOVE_TILE, D), lambda i: (i, 0)), any_spec],
        out_specs=any_spec,
        out_shape=jax.ShapeDtypeStruct(xs_zero.shape, xs_zero.dtype),
        scratch_shapes=[pltpu.SemaphoreType.DMA(())],
        input_output_aliases={2: 0},
        compiler_params=_cparams("arbitrary"),
        name="moe_dispatch",
    )(pos3, h2, xs_zero)


def _combine_kernel(pos_ref, y_hbm, h_ref, wt_ref, g_ref, b_ref, o_ref, ybuf, sem):
    def start(r, carry):
        for c in range(2):
            _row_copy(y_hbm, pos_ref[0, c, r], ybuf.at[c], r, sem).start()
        return carry

    lax.fori_loop(0, MOVE_TILE, start, 0, unroll=8)

    def wait(r, carry):
        for c in range(2):
            _row_copy(y_hbm, pos_ref[0, c, r], ybuf.at[c], r, sem).wait()
        return carry

    lax.fori_loop(0, MOVE_TILE, wait, 0, unroll=8)

    wt = wt_ref[...]
    f = wt[:, 0:1] * ybuf[0] + wt[:, 1:2] * ybuf[1]
    o_ref[...] = _layer_norm(ALPHA * h_ref[...] + f, g_ref[...], b_ref[...])


def _combine(pos3, y, h2, wt_col, g, b):
    n, D = h2.shape
    tm = MOVE_TILE
    return pl.pallas_call(
        _combine_kernel,
        grid=(n // tm,),
        in_specs=[pl.BlockSpec((1, 2, tm), lambda i: (i, 0, 0), memory_space=pltpu.SMEM),
                  pl.BlockSpec(memory_space=pl.ANY),
                  pl.BlockSpec((tm, D), lambda i: (i, 0)),
                  pl.BlockSpec((tm, 2), lambda i: (i, 0)),
                  _const_spec((1, D)), _const_spec((1, D))],
        out_specs=pl.BlockSpec((tm, D), lambda i: (i, 0)),
        out_shape=jax.ShapeDtypeStruct((n, D), F32),
        scratch_shapes=[pltpu.VMEM((2, tm, D), F32), pltpu.SemaphoreType.DMA(())],
        compiler_params=_cparams("arbitrary"),
        name="moe_combine",
    )(pos3, y, h2, wt_col, g, b)


def _moe(h2, w_router_t, wgu, wd, g, b):
    n, D = h2.shape
    tm = FFN_TILE
    idx, wt = _router(h2, w_router_t)
    rank, cnt = _rank(idx)

    counts = cnt[:, 0]
    padded = (counts + tm - 1) // tm * tm
    ends = jnp.cumsum(padded)
    starts = ends - padded
    one_hot = idx[..., None] == jnp.arange(N_EXPERTS, dtype=jnp.int32)
    pos = jnp.sum(jnp.where(one_hot, starts, 0), axis=-1) + rank
    n_tiles = (2 * n) // tm + N_EXPERTS
    n_used = (ends[-1] // tm).astype(jnp.int32)
    tile_start = jnp.arange(n_tiles, dtype=jnp.int32) * tm
    tile_expert = jnp.sum(tile_start[:, None] >= ends[None, :], axis=1).astype(jnp.int32)
    tile_expert = jnp.minimum(tile_expert, tile_expert[jnp.maximum(n_used - 1, 0)])
    pos3 = pos.reshape(2, n // MOVE_TILE, MOVE_TILE).transpose(1, 0, 2)

    xs = _dispatch(pos3, h2, jnp.zeros((n_tiles * tm, D), F32))
    y = _ffn_grouped(tile_expert, n_used.reshape(1), xs, wgu, wd)
    return _combine(pos3, y, h2, wt.T, g, b)


def _pack_ffn(wg, wu, wd):
    lead = wg.shape[:-2]
    split = lambda w: w.astype(BF16).reshape(*lead, D_MODEL, N_FF_CHUNKS, FF_CHUNK)
    wgu = jnp.concatenate([split(wg), split(wu)], axis=-1)
    wgu = jnp.moveaxis(wgu, -2, -3)
    return wgu, wd.astype(BF16).reshape(*lead, N_FF_CHUNKS, FF_CHUNK, D_MODEL)


def _pack_lru_gates(w_r, w_i):
    def pair(w):
        w = w.reshape(LRU_HEADS // 2, 2, LRU_BLOCK, LRU_BLOCK)
        z = jnp.zeros_like(w[:, 0])
        top = jnp.concatenate([w[:, 0], z], axis=-1)
        bot = jnp.concatenate([z, w[:, 1]], axis=-1)
        return jnp.concatenate([top, bot], axis=-2)
    return jnp.concatenate([pair(w_r), pair(w_i)], axis=-1).astype(BF16)


def kernel(x, meta_tokens, ln_in_g, ln_in_b, w_in, b_gate, da_lambda, da_subln_g, conv_w, conv_b, lru_wr, lru_br, lru_wi, lru_bi, lru_lambda, w_attn_out, w_lru_out, w_o, ln_g, ln_b, ffn_wg, ffn_wu, ffn_wd, router_w, moe_wg, moe_wu, moe_wd):
    B, S, D = x.shape
    tp = S + META_BLOCK
    n = B * tp
    row = lambda a: a.reshape(1, -1)

    meta_pad = jnp.pad(meta_tokens, ((META_ROW0, 0), (0, 0)))
    h = _ln_in(x, meta_pad, row(ln_in_g), row(ln_in_b)).reshape(n, D)

    w_in_b = w_in.astype(BF16)
    wa_b, wl_b, wo_b = w_attn_out.astype(BF16), w_lru_out.astype(BF16), w_o.astype(BF16)
    dense_wgu, dense_wd = _pack_ffn(ffn_wg, ffn_wu, ffn_wd)
    moe_wgu, moe_wdn = _pack_ffn(moe_wg, moe_wu, moe_wd)

    for i in range(DEPTH):
        q, k, v, xr, gr, ga, gl = _in_proj(h, w_in_b[i])
        seq3 = lambda a: a.reshape(B, tp, a.shape[-1])
        ao = _attention(seq3(q), seq3(k), seq3(v), da_lambda[i], row(jnp.tile(da_subln_g[i], 2)), i, S)
        rec = _conv_lru(seq3(xr), seq3(gr), conv_w[i], row(conv_b[i]), _pack_lru_gates(lru_wr[i], lru_wi[i]),
                        row(lru_br[i]), row(lru_bi[i]), row(lru_lambda[i]))
        h = _mix_out(ao.reshape(n, DA_WIDTH), rec.reshape(n, D), ga, gl, h, wa_b[i], wl_b[i], wo_b[i],
                     b_gate[i], row(ln_g[i, 0]), row(ln_b[i, 0]))
        if i % 2 == 0:
            h = _ffn_dense(h, dense_wgu[i // 2], dense_wd[i // 2], row(ln_g[i, 1]), row(ln_b[i, 1]))
        else:
            h = _moe(h, router_w[i // 2].T, moe_wgu[i // 2], moe_wdn[i // 2], row(ln_g[i, 1]), row(ln_b[i, 1]))
    return h.reshape(B, tp, D)[:, :S]
```

```python
import functools
import math

import jax
import jax.numpy as jnp
from jax import lax
from jax.experimental import pallas as pl
from jax.experimental.pallas import tpu as pltpu

D_MODEL = 1024
DEPTH = 4
CHUNK = 64
N_META = 16
DA_HEADS = 8
DA_HEAD_DIM = 32
DA_WIDTH = 512
LRU_WIDTH = 1024
LRU_HEADS = 16
LRU_BLOCK = 64
CONV_WIDTH = 4
LRU_C = 8.0
D_FF = 2816
N_EXPERTS = 8
ALPHA = (2 * DEPTH) ** 0.25
LN_EPS = 1e-5
RMS_EPS = 1e-5
PROJ_WIDTH = 3 * DA_WIDTH + 2 * LRU_WIDTH + 2 * D_MODEL

LANES = 128
META_BLOCK = 128
META_ROW0 = META_BLOCK - N_META
TIME_BLOCK = 128
ATTN_Q = 256
LRU_SEG = TIME_BLOCK // 8
FF_CHUNK = 1408
N_FF_CHUNKS = D_FF // FF_CHUNK
ROW_TILE = 512
FFN_TILE = 512
ROUTE_TILE = 512
MOVE_TILE = 256
NEG_BIG = -1e30
VMEM_LIMIT = 56 * 1024 * 1024

BF16 = jnp.bfloat16
F32 = jnp.float32


def _cparams(*sem):
    return pltpu.CompilerParams(dimension_semantics=sem, vmem_limit_bytes=VMEM_LIMIT)


def _layer_norm(x, g, b):
    mu = jnp.mean(x, axis=-1, keepdims=True)
    xc = x - mu
    var = jnp.mean(xc * xc, axis=-1, keepdims=True)
    return xc * lax.rsqrt(var + LN_EPS) * g + b


def _const_spec(shape):
    nd = len(shape)
    return pl.BlockSpec(shape, lambda *_: (0,) * nd)


def _ln_in_kernel(x_ref, meta_ref, g_ref, b_ref, o_ref):
    j = pl.program_id(1)
    last = pl.num_programs(1) - 1

    @pl.when(j < last)
    def _():
        o_ref[0] = _layer_norm(x_ref[0], g_ref[...], b_ref[...])

    @pl.when(j == last)
    def _():
        y = _layer_norm(meta_ref[...], g_ref[...], b_ref[...])
        row = lax.broadcasted_iota(jnp.int32, y.shape, 0)
        o_ref[0] = jnp.where(row >= META_ROW0, y, 0.0)


def _ln_in(x, meta_pad, g, b):
    B, S, D = x.shape
    nblk = S // TIME_BLOCK + 1
    return pl.pallas_call(
        _ln_in_kernel,
        grid=(B, nblk),
        in_specs=[
            pl.BlockSpec((1, TIME_BLOCK, D), lambda b_, j: (b_, jnp.minimum(j, nblk - 2), 0)),
            _const_spec((META_BLOCK, D)),
            _const_spec((1, D)),
            _const_spec((1, D)),
        ],
        out_specs=pl.BlockSpec((1, TIME_BLOCK, D), lambda b_, j: (b_, j, 0)),
        out_shape=jax.ShapeDtypeStruct((B, S + META_BLOCK, D), F32),
        compiler_params=_cparams("parallel", "arbitrary"),
        name="ln_in",
    )(x, meta_pad, g, b)


_PROJ_SEGS = (DA_WIDTH, DA_WIDTH, DA_WIDTH, LRU_WIDTH, LRU_WIDTH, D_MODEL, D_MODEL)


def _in_proj_kernel(h_ref, w_ref, q_ref, k_ref, v_ref, xr_ref, gr_ref, ga_ref, gl_ref):
    hb = h_ref[...].astype(BF16)
    outs = (q_ref, k_ref, v_ref, xr_ref, gr_ref, ga_ref, gl_ref)
    col = 0
    for idx, (o_ref, width) in enumerate(zip(outs, _PROJ_SEGS)):
        for c in range(0, width, 512):
            z = jnp.dot(hb, w_ref[:, col + c:col + c + 512], preferred_element_type=F32)
            if idx == 0:
                z = z * (DA_HEAD_DIM ** -0.5 * math.log2(math.e))
            o_ref[:, c:c + 512] = z.astype(BF16)
        col += width


def _in_proj(h2, w_in):
    n, D = h2.shape
    tm = ROW_TILE
    row = lambda w: pl.BlockSpec((tm, w), lambda i: (i, 0))
    return pl.pallas_call(
        _in_proj_kernel,
        grid=(n // tm,),
        in_specs=[row(D), _const_spec((D, PROJ_WIDTH))],
        out_specs=[row(w) for w in _PROJ_SEGS],
        out_shape=[jax.ShapeDtypeStruct((n, w), BF16) for w in _PROJ_SEGS],
        compiler_params=_cparams("parallel"),
        name="in_proj",
    )(h2, w_in)


_NT = (((1,), (1,)), ((), ()))
HEAD_LANES = 2 * DA_HEAD_DIM


def _attn_stack_queries(q, qs_scr):
    tq = q.shape[0]
    lane = lax.broadcasted_iota(jnp.int32, q.shape, 1)
    zero = jnp.zeros_like(q)
    for x in range(2):
        for m in range(2):
            qs_scr[x, m * tq:(m + 1) * tq, :] = jnp.where(lane // DA_HEAD_DIM == 2 * x + m, q, zero)


def _attn_scores(qs_scr, x, ks):
    return lax.dot_general(qs_scr[x], ks, _NT, preferred_element_type=F32)


def _row_max(s):
    return jnp.broadcast_to(jnp.max(s, axis=1, keepdims=True), (s.shape[0], LANES))


def _attn_update(m_scr, acc_scr, scores, vs, mask, first, row_max=None):
    tk = vs.shape[0]
    lane = lax.broadcasted_iota(jnp.int32, vs.shape, 1)
    one = jnp.ones_like(vs)
    rhs = (jnp.where(lane < HEAD_LANES, vs, one), jnp.where(lane < HEAD_LANES, one, vs))
    for x in range(2):
        s = scores[x]
        if mask is not None:
            s = jnp.where(mask, s, NEG_BIG)
        m_cur = _row_max(s) if row_max is None else row_max[x]
        if first:
            m_new = m_cur
        else:
            m_prev = m_scr[x]
            m_new = jnp.maximum(m_prev, m_cur)
        p = jnp.exp2(s - jnp.concatenate([m_new] * (tk // LANES), axis=1)).astype(BF16)
        pv = jnp.dot(p, rhs[x], preferred_element_type=F32)
        if first:
            acc_scr[x] = pv
        else:
            acc_scr[x] = acc_scr[x] * jnp.exp2(m_prev - m_new) + pv
        m_scr[x] = m_new


def _attn_finish(lam_ref, g_ref, acc_scr, o_ref, lam_init):
    tq = o_ref.shape[1]
    lf = lam_ref[...]
    lam = (jnp.exp(jnp.sum(lf[0:1] * lf[1:2], axis=1, keepdims=True))
           - jnp.exp(jnp.sum(lf[2:3] * lf[3:4], axis=1, keepdims=True)) + lam_init)
    diff = []
    for x in range(2):
        acc = acc_scr[x]
        o = acc / pltpu.roll(acc, HEAD_LANES, axis=1)
        diff.append(o[:tq] - lam * o[tq:])
    low = lax.broadcasted_iota(jnp.int32, (tq, LANES), 1) < HEAD_LANES
    d = jnp.where(low, diff[0], diff[1])
    dd = d * d
    ss_lo = jnp.sum(jnp.where(low, dd, 0.0), axis=1, keepdims=True)
    ss_hi = jnp.sum(jnp.where(low, 0.0, dd), axis=1, keepdims=True)
    ms = jnp.where(low, ss_lo, ss_hi) * (1.0 / HEAD_LANES)
    out = d * lax.rsqrt(ms + RMS_EPS) * g_ref[...] * (1.0 - lam_init)
    o_ref[0] = out.astype(o_ref.dtype)


def _meta_key_mask(rows):
    return lax.broadcasted_iota(jnp.int32, (rows, META_BLOCK), 1) >= META_ROW0


def _attn_kernel(lam_ref, g_ref, q_ref, k_ref, v_ref, o_in_ref, o_ref, qs_scr, m_scr, acc_scr,
                 sa_scr, sb_scr, ma_scr, mb_scr, *, seq, lam_init):
    del o_in_ref
    i = pl.program_id(2)
    tq = ATTN_Q
    _attn_stack_queries(q_ref[0], qs_scr)
    keys = lambda j: k_ref[0, pl.ds(pl.multiple_of(j * tq, tq), tq), :]
    vals = lambda j: v_ref[0, pl.ds(pl.multiple_of(j * tq, tq), tq), :]
    even = (sa_scr, ma_scr)
    odd = (sb_scr, mb_scr)

    def prefetch(j, slot):
        s_scr, mx_scr = slot
        for x in range(2):
            s = _attn_scores(qs_scr, x, keys(j))
            s_scr[x] = s
            mx_scr[x] = _row_max(s)

    prefetch(0, even)

    k0 = k_ref[0, seq:seq + META_BLOCK, :]
    _attn_update(m_scr, acc_scr, [_attn_scores(qs_scr, x, k0) for x in range(2)],
                 v_ref[0, seq:seq + META_BLOCK, :], _meta_key_mask(2 * tq), first=True)

    def full_block(j, cur, nxt):
        prefetch(j + 1, nxt)
        s_scr, mx_scr = cur
        _attn_update(m_scr, acc_scr, [s_scr[x] for x in range(2)], vals(j), None, first=False,
                     row_max=[mx_scr[x] for x in range(2)])

    def diagonal_block(cur):
        row = lax.broadcasted_iota(jnp.int32, (2 * tq, tq), 0) % tq
        col = lax.broadcasted_iota(jnp.int32, (2 * tq, tq), 1)
        _attn_update(m_scr, acc_scr, [cur[0][x] for x in range(2)], vals(i), col // CHUNK <= row // CHUNK,
                     first=False)

    def body(jj, carry):
        full_block(2 * jj, even, odd)
        full_block(2 * jj + 1, odd, even)
        return carry

    lax.fori_loop(0, i // 2, body, 0)

    @pl.when(i % 2 == 1)
    def _():
        full_block(i - 1, even, odd)
        diagonal_block(odd)

    @pl.when(i % 2 == 0)
    def _():
        diagonal_block(even)

    _attn_finish(lam_ref, g_ref, acc_scr, o_ref, lam_init)


def _attn_meta_kernel(lam_ref, g_ref, q_ref, k_ref, v_ref, o_in_ref, o_ref, qs_scr, m_scr, acc_scr, *, lam_init):
    del o_in_ref
    _attn_stack_queries(q_ref[0], qs_scr)
    _attn_update(m_scr, acc_scr, [_attn_scores(qs_scr, x, k_ref[0]) for x in range(2)], v_ref[0],
                 _meta_key_mask(2 * META_BLOCK), first=True)
    _attn_finish(lam_ref, g_ref, acc_scr, o_ref, lam_init)


def _attn_scratch(tq):
    return [pltpu.VMEM((2, 2 * tq, LANES), BF16), pltpu.VMEM((2, 2 * tq, LANES), F32),
            pltpu.VMEM((2, 2 * tq, LANES), F32)]


def _attention(q, k, v, da_lambda, subln_g2, layer, seq):
    B, tp, _ = q.shape
    lam_init = 0.8 - 0.6 * math.exp(-0.3 * layer)
    groups = DA_WIDTH // LANES
    small = [_const_spec((4, DA_HEAD_DIM)), _const_spec((1, LANES))]
    kv_spec = pl.BlockSpec((1, tp, LANES), lambda b_, g, i: (b_, 0, g))
    qo_spec = pl.BlockSpec((1, ATTN_Q, LANES), lambda b_, g, i: (b_, i, g))
    o = pl.pallas_call(
        functools.partial(_attn_kernel, seq=seq, lam_init=lam_init),
        grid=(B, groups, seq // ATTN_Q),
        in_specs=small + [qo_spec, kv_spec, kv_spec, pl.BlockSpec(memory_space=pl.ANY)],
        out_specs=qo_spec,
        out_shape=jax.ShapeDtypeStruct((B, tp, DA_WIDTH), BF16),
        scratch_shapes=(_attn_scratch(ATTN_Q) + [pltpu.VMEM((2, 2 * ATTN_Q, ATTN_Q), F32)] * 2
                        + [pltpu.VMEM((2, 2 * ATTN_Q, LANES), F32)] * 2),
        input_output_aliases={5: 0},
        compiler_params=_cparams("parallel", "parallel", "arbitrary"),
        name="diff_attn",
    )(da_lambda, subln_g2, q, k, v, jnp.zeros((B, tp, DA_WIDTH), BF16))
    meta_spec = pl.BlockSpec((1, META_BLOCK, LANES), lambda b_, g: (b_, seq // META_BLOCK, g))
    return pl.pallas_call(
        functools.partial(_attn_meta_kernel, lam_init=lam_init),
        grid=(B, groups),
        in_specs=small + [meta_spec, meta_spec, meta_spec, pl.BlockSpec(memory_space=pl.ANY)],
        out_specs=meta_spec,
        out_shape=jax.ShapeDtypeStruct((B, tp, DA_WIDTH), BF16),
        scratch_shapes=_attn_scratch(META_BLOCK),
        input_output_aliases={5: 0},
        compiler_params=_cparams("parallel", "parallel"),
        name="diff_attn_meta",
    )(da_lambda, subln_g2, q, k, v, o)


def _lru_kernel(xr_ref, gr_ref, cw_ref, cb_ref, wg_ref, br_ref, bi_ref, lam_ref, o_ref,
                xbuf, a_buf, u_buf, h_buf, p_buf, carry):
    j = pl.program_id(1)
    tt = TIME_BLOCK

    @pl.when(j == 0)
    def _():
        xbuf[0:8, :] = jnp.zeros((8, LRU_WIDTH), F32)
        carry[...] = jnp.zeros_like(carry)

    live = (lax.broadcasted_iota(jnp.int32, (tt, LANES), 0) >= META_ROW0) | (j > 0)
    for g in range(LRU_WIDTH // LANES):
        sl = slice(g * LANES, (g + 1) * LANES)
        xbuf[8:8 + tt, sl] = jnp.where(live, xr_ref[0, :, sl].astype(F32), 0.0)
    for g in range(LRU_WIDTH // LANES):
        sl = slice(g * LANES, (g + 1) * LANES)
        xc = cb_ref[:, sl] + xbuf[5:5 + tt, sl] * cw_ref[0:1, sl]
        for t in range(1, CONV_WIDTH):
            xc = xc + xbuf[5 + t:5 + t + tt, sl] * cw_ref[t:t + 1, sl]
        lam = lam_ref[:, sl]
        softplus_neg = jnp.maximum(-lam, 0.0) + jnp.log1p(jnp.exp(-jnp.abs(lam)))
        pre = jnp.dot(xc.astype(BF16), wg_ref[g], preferred_element_type=F32)
        r = jax.nn.sigmoid(pre[:, :LANES] + br_ref[:, sl])
        ig = jax.nn.sigmoid(pre[:, LANES:] + bi_ref[:, sl])
        log_a = -LRU_C * r * softplus_neg
        a_buf[g] = jnp.exp(log_a)
        th = jnp.tanh(log_a)
        mult = jnp.sqrt(-2.0 * th / (1.0 - th))
        u_buf[g] = jnp.where(live, mult * ig * xc, 0.0)
    xbuf[0:8, :] = xbuf[tt:tt + 8, :]

    for g in range(LRU_WIDTH // LANES):
        sl = slice(g * LANES, (g + 1) * LANES)
        h = jnp.zeros((8, LANES), F32)
        p = jnp.ones((8, LANES), F32)
        for t in range(LRU_SEG):
            idx = pl.ds(t, 8, stride=LRU_SEG)
            a = a_buf[g, idx, :]
            h = a * h + u_buf[g, idx, :]
            p = a * p
            h_buf[g, idx, :] = h
            p_buf[g, idx, :] = p
        c = carry[0:1, sl]
        inits = []
        for s in range(8):
            inits.append(jnp.broadcast_to(c, (LRU_SEG, LANES)))
            c = h[s:s + 1, :] + p[s:s + 1, :] * c
        carry[:, sl] = jnp.broadcast_to(c, (8, LANES))
        h_true = h_buf[g] + p_buf[g] * jnp.concatenate(inits, axis=0)
        o_ref[0, :, sl] = (h_true * jax.nn.gelu(gr_ref[0, :, sl].astype(F32))).astype(o_ref.dtype)


def _conv_lru(xr, gr, conv_w, conv_b, w_gate, b_r, b_i, lru_lambda):
    B, tp, C = xr.shape
    nblk = tp // TIME_BLOCK
    blk = pl.BlockSpec((1, TIME_BLOCK, C), lambda b_, j: (b_, (j + nblk - 1) % nblk, 0))
    vm = lambda r: pltpu.VMEM((r, C), F32)
    grp = pltpu.VMEM((C // LANES, TIME_BLOCK, LANES), F32)
    return pl.pallas_call(
        _lru_kernel,
        grid=(B, nblk),
        in_specs=[blk, blk, _const_spec((CONV_WIDTH, C)), _const_spec((1, C)),
                  _const_spec((C // LANES, LANES, 2 * LANES)), _const_spec((1, C)), _const_spec((1, C)),
                  _const_spec((1, C))],
        out_specs=blk,
        out_shape=jax.ShapeDtypeStruct((B, tp, C), BF16),
        scratch_shapes=[vm(8 + TIME_BLOCK), grp, grp, grp, grp, vm(8)],
        compiler_params=_cparams("parallel", "arbitrary"),
        name="conv_rglru",
    )(xr, gr, conv_w, conv_b, w_gate, b_r, b_i, lru_lambda)


def _mix_out_kernel(ao_ref, rec_ref, ga_ref, gl_ref, h_ref, wa_ref, wl_ref, wo_ref, bg_ref, g_ref, b_ref, o_ref):
    attn_up = jnp.dot(ao_ref[...], wa_ref[...], preferred_element_type=F32)
    lru_up = jnp.dot(rec_ref[...], wl_ref[...], preferred_element_type=F32)
    merged = (jax.nn.sigmoid(ga_ref[...].astype(F32) + bg_ref[0:1, :]) * attn_up
              + jax.nn.sigmoid(gl_ref[...].astype(F32) + bg_ref[1:2, :]) * lru_up)
    mix = jnp.dot(merged.astype(BF16), wo_ref[...], preferred_element_type=F32)
    o_ref[...] = _layer_norm(ALPHA * h_ref[...] + mix, g_ref[...], b_ref[...])


def _mix_out(ao, rec, ga, gl, h2, wa, wl, wo, bg, g, b):
    n, D = h2.shape
    tm = ROW_TILE
    row = lambda w: pl.BlockSpec((tm, w), lambda i: (i, 0))
    return pl.pallas_call(
        _mix_out_kernel,
        grid=(n // tm,),
        in_specs=[row(DA_WIDTH), row(D), row(D), row(D), row(D),
                  _const_spec((DA_WIDTH, D)), _const_spec((D, D)), _const_spec((D, D)),
                  _const_spec((2, D)), _const_spec((1, D)), _const_spec((1, D))],
        out_specs=row(D),
        out_shape=jax.ShapeDtypeStruct((n, D), F32),
        compiler_params=_cparams("parallel"),
        name="mix_out",
    )(ao, rec, ga, gl, h2, wa, wl, wo, bg, g, b)


def _swiglu_chunks(j, x_ref, wgu_ref, wd_ref, xb_scr, acc_scr, finish, enabled=True):
    last = N_FF_CHUNKS - 1

    def part():
        gu = jnp.dot(xb_scr[...], wgu_ref[...], preferred_element_type=F32)
        act = (jax.nn.silu(gu[:, :FF_CHUNK]) * gu[:, FF_CHUNK:]).astype(BF16)
        return jnp.dot(act, wd_ref[...], preferred_element_type=F32)

    @pl.when(enabled & (j == 0))
    def _():
        xb_scr[...] = x_ref[...].astype(BF16)
        acc_scr[...] = part()

    if N_FF_CHUNKS > 2:
        @pl.when(enabled & (j > 0) & (j < last))
        def _():
            acc_scr[...] += part()

    @pl.when(enabled & (j == last))
    def _():
        finish(acc_scr[...] + part())


def _ffn_dense_kernel(x_ref, wgu_ref, wd_ref, g_ref, b_ref, o_ref, xb_scr, acc_scr):
    def finish(f):
        o_ref[...] = _layer_norm(ALPHA * x_ref[...] + f, g_ref[...], b_ref[...])

    _swiglu_chunks(pl.program_id(1), x_ref, wgu_ref.at[0], wd_ref.at[0], xb_scr, acc_scr, finish)


def _ffn_dense(h2, wgu, wd, g, b):
    n, D = h2.shape
    tm = FFN_TILE
    return pl.pallas_call(
        _ffn_dense_kernel,
        grid=(n // tm, N_FF_CHUNKS),
        in_specs=[pl.BlockSpec((tm, D), lambda i, j: (i, 0)),
                  pl.BlockSpec((1, D, 2 * FF_CHUNK), lambda i, j: (j, 0, 0)),
                  pl.BlockSpec((1, FF_CHUNK, D), lambda i, j: (j, 0, 0)),
                  _const_spec((1, D)), _const_spec((1, D))],
        out_specs=pl.BlockSpec((tm, D), lambda i, j: (i, 0)),
        out_shape=jax.ShapeDtypeStruct((n, D), F32),
        scratch_shapes=[pltpu.VMEM((tm, D), BF16), pltpu.VMEM((tm, D), F32)],
        compiler_params=_cparams("parallel", "arbitrary"),
        name="ffn_dense",
    )(h2, wgu, wd, g, b)


def _ffn_group_kernel(te_ref, nu_ref, x_ref, wgu_ref, wd_ref, o_ref, xb_scr, acc_scr):
    t = pl.program_id(0)
    j = pl.program_id(1)
    used = t < nu_ref[0]

    def finish(f):
        o_ref[...] = f

    _swiglu_chunks(j, x_ref, wgu_ref.at[0, 0], wd_ref.at[0, 0], xb_scr, acc_scr, finish, enabled=used)

    @pl.when(jnp.logical_not(used) & (j == N_FF_CHUNKS - 1))
    def _():
        o_ref[...] = jnp.zeros_like(o_ref)


def _ffn_grouped(tile_expert, n_used, xs, wgu, wd):
    n, D = xs.shape
    tm = FFN_TILE
    nf = N_FF_CHUNKS

    def chunk(t, j, te, nu):
        return jnp.where(t < nu[0], j, nf - 1)

    grid_spec = pltpu.PrefetchScalarGridSpec(
        num_scalar_prefetch=2,
        grid=(n // tm, nf),
        in_specs=[pl.BlockSpec((tm, D), lambda t, j, te, nu: (jnp.minimum(t, nu[0] - 1), 0)),
                  pl.BlockSpec((1, 1, D, 2 * FF_CHUNK), lambda t, j, te, nu: (te[t], chunk(t, j, te, nu), 0, 0)),
                  pl.BlockSpec((1, 1, FF_CHUNK, D), lambda t, j, te, nu: (te[t], chunk(t, j, te, nu), 0, 0))],
        out_specs=pl.BlockSpec((tm, D), lambda t, j, te, nu: (t, 0)),
        scratch_shapes=[pltpu.VMEM((tm, D), BF16), pltpu.VMEM((tm, D), F32)],
    )
    return pl.pallas_call(
        _ffn_group_kernel,
        grid_spec=grid_spec,
        out_shape=jax.ShapeDtypeStruct((n, D), F32),
        compiler_params=_cparams("arbitrary", "arbitrary"),
        name="ffn_grouped",
    )(tile_expert, n_used, xs, wgu, wd)


def _router_kernel(h_ref, w_ref, idx_ref, wt_ref):
    nt = (((1,), (1,)), ((), ()))
    h = h_ref[...]
    w = w_ref[...]
    h_hi = h.astype(BF16)
    h_lo = (h - h_hi.astype(F32)).astype(BF16)
    w_hi = w.astype(BF16)
    w_lo = (w - w_hi.astype(F32)).astype(BF16)
    logits = (lax.dot_general(w_hi, h_hi, nt, preferred_element_type=F32)
              + lax.dot_general(w_lo, h_hi, nt, preferred_element_type=F32)
              + lax.dot_general(w_hi, h_lo, nt, preferred_element_type=F32))
    e = lax.broadcasted_iota(jnp.int32, logits.shape, 0)
    v1 = jnp.max(logits, axis=0, keepdims=True)
    i1 = jnp.min(jnp.where(logits == v1, e, N_EXPERTS), axis=0, keepdims=True)
    rest = jnp.where(e == i1, -jnp.inf, logits)
    v2 = jnp.max(rest, axis=0, keepdims=True)
    i2 = jnp.min(jnp.where(rest == v2, e, N_EXPERTS), axis=0, keepdims=True)
    ex = jnp.exp(v2 - v1)
    w1 = 1.0 / (1.0 + ex)
    idx_ref[...] = jnp.concatenate([i1, i2], axis=0)
    wt_ref[...] = jnp.concatenate([w1, ex * w1], axis=0)


def _router(h2, w_t):
    n, D = h2.shape
    tm = ROUTE_TILE
    out = pl.BlockSpec((2, tm), lambda i: (0, i))
    return pl.pallas_call(
        _router_kernel,
        grid=(n // tm,),
        in_specs=[pl.BlockSpec((tm, D), lambda i: (i, 0)), _const_spec((N_EXPERTS, D))],
        out_specs=[out, out],
        out_shape=[jax.ShapeDtypeStruct((2, n), jnp.int32), jax.ShapeDtypeStruct((2, n), F32)],
        compiler_params=_cparams("parallel"),
        name="router_top2",
    )(h2, w_t)


def _rank_kernel(idx_ref, rk_ref, cnt_ref, carry):
    i = pl.program_id(0)
    tm = ROUTE_TILE

    @pl.when(i == 0)
    def _():
        carry[...] = jnp.zeros_like(carry)

    idx = idx_ref[...]
    e = lax.broadcasted_iota(jnp.int32, (N_EXPERTS, tm), 0)
    hit0 = e == idx[0:1, :]
    hit1 = e == idx[1:2, :]
    member = (hit0 | hit1).astype(F32)
    before = (lax.broadcasted_iota(jnp.int32, (tm, tm), 0)
              < lax.broadcasted_iota(jnp.int32, (tm, tm), 1)).astype(BF16)
    rank = jnp.dot(member.astype(BF16), before, preferred_element_type=F32) + carry[:, 0:1]
    r0 = jnp.sum(jnp.where(hit0, rank, 0.0), axis=0, keepdims=True)
    r1 = jnp.sum(jnp.where(hit1, rank, 0.0), axis=0, keepdims=True)
    rk_ref[...] = jnp.concatenate([r0, r1], axis=0).astype(jnp.int32)
    total = carry[...] + jnp.sum(member, axis=1, keepdims=True)
    carry[...] = total
    cnt_ref[...] = total.astype(jnp.int32)


def _rank(idx):
    n = idx.shape[1]
    tm = ROUTE_TILE
    blk = pl.BlockSpec((2, tm), lambda i: (0, i))
    return pl.pallas_call(
        _rank_kernel,
        grid=(n // tm,),
        in_specs=[blk],
        out_specs=[blk, _const_spec((N_EXPERTS, LANES))],
        out_shape=[jax.ShapeDtypeStruct((2, n), jnp.int32), jax.ShapeDtypeStruct((N_EXPERTS, LANES), jnp.int32)],
        scratch_shapes=[pltpu.VMEM((N_EXPERTS, LANES), F32)],
        compiler_params=_cparams("arbitrary"),
        name="route_rank",
    )(idx)


def _row_copy(src, s, dst, d, sem):
    return pltpu.make_async_copy(src.at[pl.ds(s, 1)], dst.at[pl.ds(d, 1)], sem)


def _dispatch_kernel(pos_ref, h_ref, xs_in, xs_hbm, sem):
    del xs_in

    def start(r, carry):
        for c in range(2):
            _row_copy(h_ref, r, xs_hbm, pos_ref[0, c, r], sem).start(priority=c)
        return carry

    lax.fori_loop(0, MOVE_TILE, start, 0, unroll=8)

    def wait(r, carry):
        for c in range(2):
            _row_copy(h_ref, r, xs_hbm, pos_ref[0, c, r], sem).wait()
        return carry

    lax.fori_loop(0, MOVE_TILE, wait, 0, unroll=8)


def _dispatch(pos3, h2, xs_zero):
    n, D = h2.shape
    any_spec = pl.BlockSpec(memory_space=pl.ANY)
    return pl.pallas_call(
        _dispatch_kernel,
        grid=(n // MOVE_TILE,),
        in_specs=[pl.BlockSpec((1, 2, MOVE_TILE), lambda i: (i, 0, 0), memory_space=pltpu.SMEM),
                  pl.BlockSpec((MOVE_TILE, D), lambda i: (i, 0)), any_spec],
        out_specs=any_spec,
        out_shape=jax.ShapeDtypeStruct(xs_zero.shape, xs_zero.dtype),
        scratch_shapes=[pltpu.SemaphoreType.DMA(())],
        input_output_aliases={2: 0},
        compiler_params=_cparams("arbitrary"),
        name="moe_dispatch",
    )(pos3, h2, xs_zero)


def _combine_kernel(pos_ref, y_hbm, h_ref, wt_ref, g_ref, b_ref, o_ref, ybuf, sem):
    def start(r, carry):
        for c in range(2):
            _row_copy(y_hbm, pos_ref[0, c, r], ybuf.at[c], r, sem).start(priority=c)
        return carry

    lax.fori_loop(0, MOVE_TILE, start, 0, unroll=8)

    def wait(r, carry):
        for c in range(2):
            _row_copy(y_hbm, pos_ref[0, c, r], ybuf.at[c], r, sem).wait()
        return carry

    lax.fori_loop(0, MOVE_TILE, wait, 0, unroll=8)

    wt = wt_ref[...]
    f = wt[:, 0:1] * ybuf[0] + wt[:, 1:2] * ybuf[1]
    o_ref[...] = _layer_norm(ALPHA * h_ref[...] + f, g_ref[...], b_ref[...])


def _combine(pos3, y, h2, wt_col, g, b):
    n, D = h2.shape
    tm = MOVE_TILE
    return pl.pallas_call(
        _combine_kernel,
        grid=(n // tm,),
        in_specs=[pl.BlockSpec((1, 2, tm), lambda i: (i, 0, 0), memory_space=pltpu.SMEM),
                  pl.BlockSpec(memory_space=pl.ANY),
                  pl.BlockSpec((tm, D), lambda i: (i, 0)),
                  pl.BlockSpec((tm, 2), lambda i: (i, 0)),
                  _const_spec((1, D)), _const_spec((1, D))],
        out_specs=pl.BlockSpec((tm, D), lambda i: (i, 0)),
        out_shape=jax.ShapeDtypeStruct((n, D), F32),
        scratch_shapes=[pltpu.VMEM((2, tm, D), F32), pltpu.SemaphoreType.DMA(())],
        compiler_params=_cparams("arbitrary"),
        name="moe_combine",
    )(pos3, y, h2, wt_col, g, b)


def _moe(h2, w_router_t, wgu, wd, g, b):
    n, D = h2.shape
    tm = FFN_TILE
    idx, wt = _router(h2, w_router_t)
    rank, cnt = _rank(idx)

    counts = cnt[:, 0]
    padded = (counts + tm - 1) // tm * tm
    ends = jnp.cumsum(padded)
    starts = ends - padded
    one_hot = idx[..., None] == jnp.arange(N_EXPERTS, dtype=jnp.int32)
    pos = jnp.sum(jnp.where(one_hot, starts, 0), axis=-1) + rank
    n_tiles = (2 * n) // tm + N_EXPERTS
    n_used = (ends[-1] // tm).astype(jnp.int32)
    tile_start = jnp.arange(n_tiles, dtype=jnp.int32) * tm
    tile_expert = jnp.sum(tile_start[:, None] >= ends[None, :], axis=1).astype(jnp.int32)
    tile_expert = jnp.minimum(tile_expert, tile_expert[jnp.maximum(n_used - 1, 0)])
    pos3 = pos.reshape(2, n // MOVE_TILE, MOVE_TILE).transpose(1, 0, 2)

    xs = _dispatch(pos3, h2, jnp.zeros((n_tiles * tm, D), F32))
    y = _ffn_grouped(tile_expert, n_used.reshape(1), xs, wgu, wd)
    return _combine(pos3, y, h2, wt.T, g, b)


def _pack_ffn(wg, wu, wd):
    lead = wg.shape[:-2]
    split = lambda w: w.astype(BF16).reshape(*lead, D_MODEL, N_FF_CHUNKS, FF_CHUNK)
    wgu = jnp.concatenate([split(wg), split(wu)], axis=-1)
    wgu = jnp.moveaxis(wgu, -2, -3)
    return wgu, wd.astype(BF16).reshape(*lead, N_FF_CHUNKS, FF_CHUNK, D_MODEL)


def _pack_lru_gates(w_r, w_i):
    def pair(w):
        w = w.reshape(LRU_HEADS // 2, 2, LRU_BLOCK, LRU_BLOCK)
        z = jnp.zeros_like(w[:, 0])
        top = jnp.concatenate([w[:, 0], z], axis=-1)
        bot = jnp.concatenate([z, w[:, 1]], axis=-1)
        return jnp.concatenate([top, bot], axis=-2)
    return jnp.concatenate([pair(w_r), pair(w_i)], axis=-1).astype(BF16)


def kernel(x, meta_tokens, ln_in_g, ln_in_b, w_in, b_gate, da_lambda, da_subln_g, conv_w, conv_b, lru_wr, lru_br, lru_wi, lru_bi, lru_lambda, w_attn_out, w_lru_out, w_o, ln_g, ln_b, ffn_wg, ffn_wu, ffn_wd, router_w, moe_wg, moe_wu, moe_wd):
    B, S, D = x.shape
    tp = S + META_BLOCK
    n = B * tp
    row = lambda a: a.reshape(1, -1)

    meta_pad = jnp.pad(meta_tokens, ((META_ROW0, 0), (0, 0)))
    h = _ln_in(x, meta_pad, row(ln_in_g), row(ln_in_b)).reshape(n, D)

    w_in_b = w_in.astype(BF16)
    wa_b, wl_b, wo_b = w_attn_out.astype(BF16), w_lru_out.astype(BF16), w_o.astype(BF16)
    dense_wgu, dense_wd = _pack_ffn(ffn_wg, ffn_wu, ffn_wd)
    moe_wgu, moe_wdn = _pack_ffn(moe_wg, moe_wu, moe_wd)

    for i in range(DEPTH):
        q, k, v, xr, gr, ga, gl = _in_proj(h, w_in_b[i])
        seq3 = lambda a: a.reshape(B, tp, a.shape[-1])
        ao = _attention(seq3(q), seq3(k), seq3(v), da_lambda[i], row(jnp.tile(da_subln_g[i], 2)), i, S)
        rec = _conv_lru(seq3(xr), seq3(gr), conv_w[i], row(conv_b[i]), _pack_lru_gates(lru_wr[i], lru_wi[i]),
                        row(lru_br[i]), row(lru_bi[i]), row(lru_lambda[i]))
        h = _mix_out(ao.reshape(n, DA_WIDTH), rec.reshape(n, D), ga, gl, h, wa_b[i], wl_b[i], wo_b[i],
                     b_gate[i], row(ln_g[i, 0]), row(ln_b[i, 0]))
        if i % 2 == 0:
            h = _ffn_dense(h, dense_wgu[i // 2], dense_wd[i // 2], row(ln_g[i, 1]), row(ln_b[i, 1]))
        else:
            h = _moe(h, router_w[i // 2].T, moe_wgu[i // 2], moe_wdn[i // 2], row(ln_g[i, 1]), row(ln_b[i, 1]))
    return h.reshape(B, tp, D)[:, :S]
```

```python
import functools
import math

import jax
import jax.numpy as jnp
from jax import lax
from jax.experimental import pallas as pl
from jax.experimental.pallas import tpu as pltpu

D_MODEL = 1024
DEPTH = 4
CHUNK = 64
N_META = 16
DA_HEADS = 8
DA_HEAD_DIM = 32
DA_WIDTH = 512
LRU_WIDTH = 1024
LRU_HEADS = 16
LRU_BLOCK = 64
CONV_WIDTH = 4
LRU_C = 8.0
D_FF = 2816
N_EXPERTS = 8
ALPHA = (2 * DEPTH) ** 0.25
LN_EPS = 1e-5
RMS_EPS = 1e-5
PROJ_WIDTH = 3 * DA_WIDTH + 2 * LRU_WIDTH + 2 * D_MODEL

LANES = 128
META_BLOCK = 128
META_ROW0 = META_BLOCK - N_META
TIME_BLOCK = 128
ATTN_Q = 256
LRU_SEG = 12
LRU_ROWS = 8 * LRU_SEG
FF_CHUNK = 1408
N_FF_CHUNKS = D_FF // FF_CHUNK
ROW_TILE = 512
FFN_TILE = 512
ROUTE_TILE = 512
MOVE_TILE = 256
NEG_BIG = -1e30
VMEM_LIMIT = 56 * 1024 * 1024

BF16 = jnp.bfloat16
F32 = jnp.float32


def _cparams(*sem):
    return pltpu.CompilerParams(dimension_semantics=sem, vmem_limit_bytes=VMEM_LIMIT)


def _layer_norm(x, g, b):
    mu = jnp.mean(x, axis=-1, keepdims=True)
    xc = x - mu
    var = jnp.mean(xc * xc, axis=-1, keepdims=True)
    return xc * lax.rsqrt(var + LN_EPS) * g + b


def _const_spec(shape):
    nd = len(shape)
    return pl.BlockSpec(shape, lambda *_: (0,) * nd)


def _ln_in_kernel(x_ref, meta_ref, g_ref, b_ref, o_ref):
    j = pl.program_id(1)
    last = pl.num_programs(1) - 1

    @pl.when(j < last)
    def _():
        o_ref[0] = _layer_norm(x_ref[0], g_ref[...], b_ref[...])

    @pl.when(j == last)
    def _():
        y = _layer_norm(meta_ref[...], g_ref[...], b_ref[...])
        row = lax.broadcasted_iota(jnp.int32, y.shape, 0)
        o_ref[0] = jnp.where(row >= META_ROW0, y, 0.0)


def _ln_in(x, meta_pad, g, b):
    B, S, D = x.shape
    nblk = S // TIME_BLOCK + 1
    return pl.pallas_call(
        _ln_in_kernel,
        grid=(B, nblk),
        in_specs=[
            pl.BlockSpec((1, TIME_BLOCK, D), lambda b_, j: (b_, jnp.minimum(j, nblk - 2), 0)),
            _const_spec((META_BLOCK, D)),
            _const_spec((1, D)),
            _const_spec((1, D)),
        ],
        out_specs=pl.BlockSpec((1, TIME_BLOCK, D), lambda b_, j: (b_, j, 0)),
        out_shape=jax.ShapeDtypeStruct((B, S + META_BLOCK, D), F32),
        compiler_params=_cparams("parallel", "arbitrary"),
        name="ln_in",
    )(x, meta_pad, g, b)


_PROJ_SEGS = (DA_WIDTH, DA_WIDTH, DA_WIDTH, LRU_WIDTH, LRU_WIDTH, D_MODEL, D_MODEL)


def _in_proj_kernel(h_ref, w_ref, q_ref, k_ref, v_ref, xr_ref, gr_ref, ga_ref, gl_ref):
    hb = h_ref[...].astype(BF16)
    outs = (q_ref, k_ref, v_ref, xr_ref, gr_ref, ga_ref, gl_ref)
    col = 0
    for idx, (o_ref, width) in enumerate(zip(outs, _PROJ_SEGS)):
        for c in range(0, width, 512):
            z = jnp.dot(hb, w_ref[:, col + c:col + c + 512], preferred_element_type=F32)
            if idx == 0:
                z = z * (DA_HEAD_DIM ** -0.5 * math.log2(math.e))
            o_ref[:, c:c + 512] = z.astype(BF16)
        col += width


def _in_proj(h2, w_in):
    n, D = h2.shape
    tm = ROW_TILE
    row = lambda w: pl.BlockSpec((tm, w), lambda i: (i, 0))
    return pl.pallas_call(
        _in_proj_kernel,
        grid=(n // tm,),
        in_specs=[row(D), _const_spec((D, PROJ_WIDTH))],
        out_specs=[row(w) for w in _PROJ_SEGS],
        out_shape=[jax.ShapeDtypeStruct((n, w), BF16) for w in _PROJ_SEGS],
        compiler_params=_cparams("parallel"),
        name="in_proj",
    )(h2, w_in)


_NT = (((1,), (1,)), ((), ()))
HEAD_LANES = 2 * DA_HEAD_DIM


ATTN_PAIRS = 1
ATTN_STREAMS = 2 * ATTN_PAIRS


def _pair_lanes(st):
    pr = st // 2
    return slice(pr * LANES, (pr + 1) * LANES)


def _attn_stack_queries(q, qs_scr):
    tq = q.shape[0]
    lane = lax.broadcasted_iota(jnp.int32, (tq, LANES), 1)
    for st in range(ATTN_STREAMS):
        qp = q[:, _pair_lanes(st)]
        for m in range(2):
            keep = lane // DA_HEAD_DIM == 2 * (st % 2) + m
            qs_scr[st, m * tq:(m + 1) * tq, :] = jnp.where(keep, qp, jnp.zeros_like(qp))


def _attn_scores(qs_scr, st, ks):
    return lax.dot_general(qs_scr[st], ks[:, _pair_lanes(st)], _NT, preferred_element_type=F32)


def _row_max(s):
    return jnp.broadcast_to(jnp.max(s, axis=1, keepdims=True), (s.shape[0], LANES))


def _attn_update(m_scr, acc_scr, scores, vs, mask, first, row_max=None):
    tk = vs.shape[0]
    low = lax.broadcasted_iota(jnp.int32, (tk, LANES), 1) < HEAD_LANES
    for st in range(ATTN_STREAMS):
        vp = vs[:, _pair_lanes(st)]
        own = low if st % 2 == 0 else jnp.logical_not(low)
        rhs = jnp.where(own, vp, jnp.ones_like(vp))
        s = scores[st]
        if mask is not None:
            s = jnp.where(mask, s, NEG_BIG)
        m_cur = _row_max(s) if row_max is None else row_max[st]
        if first:
            m_new = m_cur
        else:
            m_prev = m_scr[st]
            m_new = jnp.maximum(m_prev, m_cur)
        p = jnp.exp2((s - jnp.concatenate([m_new] * (tk // LANES), axis=1)).astype(BF16))
        pv = jnp.dot(p, rhs, preferred_element_type=F32)
        if first:
            acc_scr[st] = pv
        else:
            acc_scr[st] = acc_scr[st] * jnp.exp2(m_prev - m_new) + pv
        m_scr[st] = m_new


def _attn_finish(lam_ref, g_ref, acc_scr, o_ref, lam_init):
    tq = o_ref.shape[1]
    lf = lam_ref[...]
    lam = (jnp.exp(jnp.sum(lf[0:1] * lf[1:2], axis=1, keepdims=True))
           - jnp.exp(jnp.sum(lf[2:3] * lf[3:4], axis=1, keepdims=True)) + lam_init)
    low = lax.broadcasted_iota(jnp.int32, (tq, LANES), 1) < HEAD_LANES
    for pr in range(ATTN_PAIRS):
        diff = []
        for x in range(2):
            acc = acc_scr[2 * pr + x]
            o = acc / pltpu.roll(acc, HEAD_LANES, axis=1)
            diff.append(o[:tq] - lam * o[tq:])
        d = jnp.where(low, diff[0], diff[1])
        dd = d * d
        ss_lo = jnp.sum(jnp.where(low, dd, 0.0), axis=1, keepdims=True)
        ss_hi = jnp.sum(jnp.where(low, 0.0, dd), axis=1, keepdims=True)
        ms = jnp.where(low, ss_lo, ss_hi) * (1.0 / HEAD_LANES)
        out = d * lax.rsqrt(ms + RMS_EPS) * g_ref[...] * (1.0 - lam_init)
        o_ref[0, :, pr * LANES:(pr + 1) * LANES] = out.astype(o_ref.dtype)


def _meta_key_mask(rows):
    return lax.broadcasted_iota(jnp.int32, (rows, META_BLOCK), 1) >= META_ROW0


def _attn_kernel(lam_ref, g_ref, q_ref, k_ref, v_ref, o_in_ref, o_ref, qs_scr, m_scr, acc_scr,
                 sa_scr, sb_scr, ma_scr, mb_scr, *, seq, lam_init):
    del o_in_ref
    i = pl.program_id(2)
    tq = ATTN_Q
    streams = range(ATTN_STREAMS)
    _attn_stack_queries(q_ref[0], qs_scr)
    keys = lambda j: k_ref[0, pl.ds(pl.multiple_of(j * tq, tq), tq), :]
    vals = lambda j: v_ref[0, pl.ds(pl.multiple_of(j * tq, tq), tq), :]
    even = (sa_scr, ma_scr)
    odd = (sb_scr, mb_scr)

    def prefetch(j, slot):
        s_scr, mx_scr = slot
        ks = keys(j)
        for st in streams:
            s = _attn_scores(qs_scr, st, ks)
            s_scr[st] = s
            mx_scr[st] = _row_max(s)

    prefetch(0, even)

    k0 = k_ref[0, seq:seq + META_BLOCK, :]
    _attn_update(m_scr, acc_scr, [_attn_scores(qs_scr, st, k0) for st in streams],
                 v_ref[0, seq:seq + META_BLOCK, :], _meta_key_mask(2 * tq), first=True)

    def full_block(j, cur, nxt):
        prefetch(j + 1, nxt)
        s_scr, mx_scr = cur
        _attn_update(m_scr, acc_scr, [s_scr[st] for st in streams], vals(j), None, first=False,
                     row_max=[mx_scr[st] for st in streams])

    def diagonal_block(cur):
        row = lax.broadcasted_iota(jnp.int32, (2 * tq, tq), 0) % tq
        col = lax.broadcasted_iota(jnp.int32, (2 * tq, tq), 1)
        _attn_update(m_scr, acc_scr, [cur[0][st] for st in streams], vals(i), col // CHUNK <= row // CHUNK,
                     first=False)

    def body(jj, carry):
        full_block(2 * jj, even, odd)
        full_block(2 * jj + 1, odd, even)
        return carry

    lax.fori_loop(0, i // 2, body, 0)

    @pl.when(i % 2 == 1)
    def _():
        full_block(i - 1, even, odd)
        diagonal_block(odd)

    @pl.when(i % 2 == 0)
    def _():
        diagonal_block(even)

    _attn_finish(lam_ref, g_ref, acc_scr, o_ref, lam_init)


def _attn_meta_kernel(lam_ref, g_ref, q_ref, k_ref, v_ref, o_in_ref, o_ref, qs_scr, m_scr, acc_scr, *, lam_init):
    del o_in_ref
    _attn_stack_queries(q_ref[0], qs_scr)
    k0 = k_ref[0]
    _attn_update(m_scr, acc_scr, [_attn_scores(qs_scr, st, k0) for st in range(ATTN_STREAMS)], v_ref[0],
                 _meta_key_mask(2 * META_BLOCK), first=True)
    _attn_finish(lam_ref, g_ref, acc_scr, o_ref, lam_init)


def _attn_scratch(tq):
    state = lambda dt: pltpu.VMEM((ATTN_STREAMS, 2 * tq, LANES), dt)
    return [state(BF16), state(F32), state(F32)]


def _attention(q, k, v, da_lambda, subln_g2, layer, seq):
    B, tp, _ = q.shape
    lam_init = 0.8 - 0.6 * math.exp(-0.3 * layer)
    width = ATTN_PAIRS * LANES
    groups = DA_WIDTH // width
    small = [_const_spec((4, DA_HEAD_DIM)), _const_spec((1, LANES))]
    kv_spec = pl.BlockSpec((1, tp, width), lambda b_, g, i: (b_, 0, g))
    qo_spec = pl.BlockSpec((1, ATTN_Q, width), lambda b_, g, i: (b_, i, g))
    o = pl.pallas_call(
        functools.partial(_attn_kernel, seq=seq, lam_init=lam_init),
        grid=(B, groups, seq // ATTN_Q),
        in_specs=small + [qo_spec, kv_spec, kv_spec, pl.BlockSpec(memory_space=pl.ANY)],
        out_specs=qo_spec,
        out_shape=jax.ShapeDtypeStruct((B, tp, DA_WIDTH), BF16),
        scratch_shapes=(_attn_scratch(ATTN_Q) + [pltpu.VMEM((ATTN_STREAMS, 2 * ATTN_Q, ATTN_Q), F32)] * 2
                        + [pltpu.VMEM((ATTN_STREAMS, 2 * ATTN_Q, LANES), F32)] * 2),
        input_output_aliases={5: 0},
        compiler_params=_cparams("parallel", "parallel", "arbitrary"),
        name="diff_attn",
    )(da_lambda, subln_g2, q, k, v, jnp.zeros((B, tp, DA_WIDTH), BF16))
    meta_spec = pl.BlockSpec((1, META_BLOCK, width), lambda b_, g: (b_, seq // META_BLOCK, g))
    return pl.pallas_call(
        functools.partial(_attn_meta_kernel, lam_init=lam_init),
        grid=(B, groups),
        in_specs=small + [meta_spec, meta_spec, meta_spec, pl.BlockSpec(memory_space=pl.ANY)],
        out_specs=meta_spec,
        out_shape=jax.ShapeDtypeStruct((B, tp, DA_WIDTH), BF16),
        scratch_shapes=_attn_scratch(META_BLOCK),
        input_output_aliases={5: 0},
        compiler_params=_cparams("parallel", "parallel"),
        name="diff_attn_meta",
    )(da_lambda, subln_g2, q, k, v, o)


def _lru_kernel(xr_ref, gr_ref, cw_ref, cb_ref, wg_ref, br_ref, bi_ref, lam_ref, o_ref,
                x_buf, h_buf, tail, carry):
    j = pl.program_id(1)
    seg = LRU_SEG
    seg_rows = lambda t: pl.ds(t, 8, stride=seg)

    @pl.when(j == 0)
    def _():
        tail[...] = jnp.zeros_like(tail)
        carry[...] = jnp.zeros_like(carry)

    sub = lax.broadcasted_iota(jnp.int32, (8, LANES), 0)
    live = [(sub * seg + t >= LRU_ROWS - N_META) | (j > 0) for t in range(seg)]

    for g in range(LRU_WIDTH // LANES):
        sl = slice(g * LANES, (g + 1) * LANES)
        x_buf[g] = xr_ref[0, :, sl].astype(F32)
    for g in range(LRU_WIDTH // LANES):
        sl = slice(g * LANES, (g + 1) * LANES)
        xs = [jnp.where(live[t], x_buf[g, seg_rows(t), :], 0.0) for t in range(seg)]
        before = {}
        for d in range(1, CONV_WIDTH):
            before[-d] = jnp.where(sub == 0, pltpu.roll(tail[d - 1, :, sl], 1, axis=0),
                                   pltpu.roll(xs[seg - d], 1, axis=0))
            tail[d - 1, :, sl] = xs[seg - d]
        at = lambda t: xs[t] if t >= 0 else before[t]
        xc = []
        for t in range(seg):
            y = cb_ref[:, sl] + at(t - CONV_WIDTH + 1) * cw_ref[0:1, sl]
            for tap in range(1, CONV_WIDTH):
                y = y + at(t - CONV_WIDTH + 1 + tap) * cw_ref[tap:tap + 1, sl]
            xc.append(y)
        xc = jnp.concatenate(xc, axis=0)
        lam = lam_ref[:, sl]
        softplus_neg = jnp.maximum(-lam, 0.0) + jnp.log1p(jnp.exp(-jnp.abs(lam)))
        pre = jnp.dot(xc.astype(BF16), wg_ref[g], preferred_element_type=F32)
        r = jax.nn.sigmoid(pre[:, :LANES] + br_ref[:, sl])
        ig = jax.nn.sigmoid(pre[:, LANES:] + bi_ref[:, sl])
        log_a = -LRU_C * r * softplus_neg
        a = jnp.exp(log_a)
        th = jnp.tanh(log_a)
        u = jnp.sqrt(-2.0 * th / (1.0 - th)) * ig * xc

        h = jnp.zeros((8, LANES), F32)
        p = jnp.ones((8, LANES), F32)
        hs, ps = [], []
        for t in range(seg):
            a_t = a[8 * t:8 * t + 8]
            h = a_t * h + jnp.where(live[t], u[8 * t:8 * t + 8], 0.0)
            p = a_t * p
            hs.append(h)
            ps.append(p)
        c = carry[0:1, sl]
        inits = []
        for s in range(8):
            inits.append(c)
            c = h[s:s + 1, :] + p[s:s + 1, :] * c
        carry[:, sl] = jnp.broadcast_to(c, (8, LANES))
        init = jnp.concatenate(inits, axis=0)
        for t in range(seg):
            h_buf[g, seg_rows(t), :] = hs[t] + ps[t] * init
        o_ref[0, :, sl] = (h_buf[g] * jax.nn.gelu(gr_ref[0, :, sl].astype(F32))).astype(o_ref.dtype)


def _conv_lru(xr, gr, conv_w, conv_b, w_gate, b_r, b_i, lru_lambda):
    B, tp, C = xr.shape
    assert tp % LRU_ROWS == 0, (tp, LRU_ROWS)
    nblk = tp // LRU_ROWS
    blk = pl.BlockSpec((1, LRU_ROWS, C), lambda b_, j: (b_, (j + nblk - 1) % nblk, 0))
    vm = lambda r: pltpu.VMEM((r, C), F32)
    grp = pltpu.VMEM((C // LANES, LRU_ROWS, LANES), F32)
    return pl.pallas_call(
        _lru_kernel,
        grid=(B, nblk),
        in_specs=[blk, blk, _const_spec((CONV_WIDTH, C)), _const_spec((1, C)),
                  _const_spec((C // LANES, LANES, 2 * LANES)), _const_spec((1, C)), _const_spec((1, C)),
                  _const_spec((1, C))],
        out_specs=blk,
        out_shape=jax.ShapeDtypeStruct((B, tp, C), BF16),
        scratch_shapes=[grp, grp, pltpu.VMEM((CONV_WIDTH - 1, 8, C), F32), vm(8)],
        compiler_params=_cparams("parallel", "arbitrary"),
        name="conv_rglru",
    )(xr, gr, conv_w, conv_b, w_gate, b_r, b_i, lru_lambda)


def _mix_out_kernel(ao_ref, rec_ref, ga_ref, gl_ref, h_ref, wa_ref, wl_ref, wo_ref, bg_ref, g_ref, b_ref, o_ref):
    attn_up = jnp.dot(ao_ref[...], wa_ref[...], preferred_element_type=F32)
    lru_up = jnp.dot(rec_ref[...], wl_ref[...], preferred_element_type=F32)
    merged = (jax.nn.sigmoid(ga_ref[...].astype(F32) + bg_ref[0:1, :]) * attn_up
              + jax.nn.sigmoid(gl_ref[...].astype(F32) + bg_ref[1:2, :]) * lru_up)
    mix = jnp.dot(merged.astype(BF16), wo_ref[...], preferred_element_type=F32)
    o_ref[...] = _layer_norm(ALPHA * h_ref[...] + mix, g_ref[...], b_ref[...])


def _mix_out(ao, rec, ga, gl, h2, wa, wl, wo, bg, g, b):
    n, D = h2.shape
    tm = ROW_TILE
    row = lambda w: pl.BlockSpec((tm, w), lambda i: (i, 0))
    return pl.pallas_call(
        _mix_out_kernel,
        grid=(n // tm,),
        in_specs=[row(DA_WIDTH), row(D), row(D), row(D), row(D),
                  _const_spec((DA_WIDTH, D)), _const_spec((D, D)), _const_spec((D, D)),
                  _const_spec((2, D)), _const_spec((1, D)), _const_spec((1, D))],
        out_specs=row(D),
        out_shape=jax.ShapeDtypeStruct((n, D), F32),
        compiler_params=_cparams("parallel"),
        name="mix_out",
    )(ao, rec, ga, gl, h2, wa, wl, wo, bg, g, b)


def _swiglu_chunks(j, x_ref, wgu_ref, wd_ref, xb_scr, acc_scr, finish, enabled=True):
    last = N_FF_CHUNKS - 1

    def part():
        gu = jnp.dot(xb_scr[...], wgu_ref[...], preferred_element_type=F32)
        act = (jax.nn.silu(gu[:, :FF_CHUNK]) * gu[:, FF_CHUNK:]).astype(BF16)
        return jnp.dot(act, wd_ref[...], preferred_element_type=F32)

    @pl.when(enabled & (j == 0))
    def _():
        xb_scr[...] = x_ref[...].astype(BF16)
        acc_scr[...] = part()

    if N_FF_CHUNKS > 2:
        @pl.when(enabled & (j > 0) & (j < last))
        def _():
            acc_scr[...] += part()

    @pl.when(enabled & (j == last))
    def _():
        finish(acc_scr[...] + part())


def _ffn_dense_kernel(x_ref, wgu_ref, wd_ref, g_ref, b_ref, o_ref, xb_scr, acc_scr):
    def finish(f):
        o_ref[...] = _layer_norm(ALPHA * x_ref[...] + f, g_ref[...], b_ref[...])

    _swiglu_chunks(pl.program_id(1), x_ref, wgu_ref.at[0], wd_ref.at[0], xb_scr, acc_scr, finish)


def _ffn_dense(h2, wgu, wd, g, b):
    n, D = h2.shape
    tm = FFN_TILE
    return pl.pallas_call(
        _ffn_dense_kernel,
        grid=(n // tm, N_FF_CHUNKS),
        in_specs=[pl.BlockSpec((tm, D), lambda i, j: (i, 0)),
                  pl.BlockSpec((1, D, 2 * FF_CHUNK), lambda i, j: (j, 0, 0)),
                  pl.BlockSpec((1, FF_CHUNK, D), lambda i, j: (j, 0, 0)),
                  _const_spec((1, D)), _const_spec((1, D))],
        out_specs=pl.BlockSpec((tm, D), lambda i, j: (i, 0)),
        out_shape=jax.ShapeDtypeStruct((n, D), F32),
        scratch_shapes=[pltpu.VMEM((tm, D), BF16), pltpu.VMEM((tm, D), F32)],
        compiler_params=_cparams("parallel", "arbitrary"),
        name="ffn_dense",
    )(h2, wgu, wd, g, b)


def _ffn_group_kernel(te_ref, nu_ref, x_ref, wgu_ref, wd_ref, o_ref, xb_scr, acc_scr):
    t = pl.program_id(0)
    j = pl.program_id(1)
    used = t < nu_ref[0]

    def finish(f):
        o_ref[...] = f

    _swiglu_chunks(j, x_ref, wgu_ref.at[0, 0], wd_ref.at[0, 0], xb_scr, acc_scr, finish, enabled=used)

    @pl.when(jnp.logical_not(used) & (j == N_FF_CHUNKS - 1))
    def _():
        o_ref[...] = jnp.zeros_like(o_ref)


def _ffn_grouped(tile_expert, n_used, xs, wgu, wd):
    n, D = xs.shape
    tm = FFN_TILE
    nf = N_FF_CHUNKS

    def chunk(t, j, te, nu):
        return jnp.where(t < nu[0], j, nf - 1)

    grid_spec = pltpu.PrefetchScalarGridSpec(
        num_scalar_prefetch=2,
        grid=(n // tm, nf),
        in_specs=[pl.BlockSpec((tm, D), lambda t, j, te, nu: (jnp.minimum(t, nu[0] - 1), 0)),
                  pl.BlockSpec((1, 1, D, 2 * FF_CHUNK), lambda t, j, te, nu: (te[t], chunk(t, j, te, nu), 0, 0)),
                  pl.BlockSpec((1, 1, FF_CHUNK, D), lambda t, j, te, nu: (te[t], chunk(t, j, te, nu), 0, 0))],
        out_specs=pl.BlockSpec((tm, D), lambda t, j, te, nu: (t, 0)),
        scratch_shapes=[pltpu.VMEM((tm, D), BF16), pltpu.VMEM((tm, D), F32)],
    )
    return pl.pallas_call(
        _ffn_group_kernel,
        grid_spec=grid_spec,
        out_shape=jax.ShapeDtypeStruct((n, D), F32),
        compiler_params=_cparams("arbitrary", "arbitrary"),
        name="ffn_grouped",
    )(tile_expert, n_used, xs, wgu, wd)


def _router_kernel(h_ref, w_ref, idx_ref, wt_ref):
    nt = (((1,), (1,)), ((), ()))
    h = h_ref[...]
    w = w_ref[...]
    h_hi = h.astype(BF16)
    h_lo = (h - h_hi.astype(F32)).astype(BF16)
    w_hi = w.astype(BF16)
    w_lo = (w - w_hi.astype(F32)).astype(BF16)
    logits = (lax.dot_general(w_hi, h_hi, nt, preferred_element_type=F32)
              + lax.dot_general(w_lo, h_hi, nt, preferred_element_type=F32)
              + lax.dot_general(w_hi, h_lo, nt, preferred_element_type=F32))
    e = lax.broadcasted_iota(jnp.int32, logits.shape, 0)
    v1 = jnp.max(logits, axis=0, keepdims=True)
    i1 = jnp.min(jnp.where(logits == v1, e, N_EXPERTS), axis=0, keepdims=True)
    rest = jnp.where(e == i1, -jnp.inf, logits)
    v2 = jnp.max(rest, axis=0, keepdims=True)
    i2 = jnp.min(jnp.where(rest == v2, e, N_EXPERTS), axis=0, keepdims=True)
    ex = jnp.exp(v2 - v1)
    w1 = 1.0 / (1.0 + ex)
    idx_ref[...] = jnp.concatenate([i1, i2], axis=0)
    wt_ref[...] = jnp.concatenate([w1, ex * w1], axis=0)


def _router(h2, w_t):
    n, D = h2.shape
    tm = ROUTE_TILE
    out = pl.BlockSpec((2, tm), lambda i: (0, i))
    return pl.pallas_call(
        _router_kernel,
        grid=(n // tm,),
        in_specs=[pl.BlockSpec((tm, D), lambda i: (i, 0)), _const_spec((N_EXPERTS, D))],
        out_specs=[out, out],
        out_shape=[jax.ShapeDtypeStruct((2, n), jnp.int32), jax.ShapeDtypeStruct((2, n), F32)],
        compiler_params=_cparams("parallel"),
        name="router_top2",
    )(h2, w_t)


def _rank_kernel(idx_ref, rk_ref, cnt_ref, carry):
    i = pl.program_id(0)
    tm = ROUTE_TILE

    @pl.when(i == 0)
    def _():
        carry[...] = jnp.zeros_like(carry)

    idx = idx_ref[...]
    e = lax.broadcasted_iota(jnp.int32, (N_EXPERTS, tm), 0)
    hit0 = e == idx[0:1, :]
    hit1 = e == idx[1:2, :]
    member = (hit0 | hit1).astype(F32)
    before = (lax.broadcasted_iota(jnp.int32, (tm, tm), 0)
              < lax.broadcasted_iota(jnp.int32, (tm, tm), 1)).astype(BF16)
    rank = jnp.dot(member.astype(BF16), before, preferred_element_type=F32) + carry[:, 0:1]
    r0 = jnp.sum(jnp.where(hit0, rank, 0.0), axis=0, keepdims=True)
    r1 = jnp.sum(jnp.where(hit1, rank, 0.0), axis=0, keepdims=True)
    rk_ref[...] = jnp.concatenate([r0, r1], axis=0).astype(jnp.int32)
    total = carry[...] + jnp.sum(member, axis=1, keepdims=True)
    carry[...] = total
    cnt_ref[...] = total.astype(jnp.int32)


def _rank(idx):
    n = idx.shape[1]
    tm = ROUTE_TILE
    blk = pl.BlockSpec((2, tm), lambda i: (0, i))
    return pl.pallas_call(
        _rank_kernel,
        grid=(n // tm,),
        in_specs=[blk],
        out_specs=[blk, _const_spec((N_EXPERTS, LANES))],
        out_shape=[jax.ShapeDtypeStruct((2, n), jnp.int32), jax.ShapeDtypeStruct((N_EXPERTS, LANES), jnp.int32)],
        scratch_shapes=[pltpu.VMEM((N_EXPERTS, LANES), F32)],
        compiler_params=_cparams("arbitrary"),
        name="route_rank",
    )(idx)


def _row_copy(src, s, dst, d, sem):
    return pltpu.make_async_copy(src.at[pl.ds(s, 1)], dst.at[pl.ds(d, 1)], sem)


def _dispatch_kernel(pos_ref, h_ref, xs_in, xs_hbm, sem):
    del xs_in

    def start(r, carry):
        for c in range(2):
            _row_copy(h_ref, r, xs_hbm, pos_ref[0, c, r], sem).start()
        return carry

    lax.fori_loop(0, MOVE_TILE, start, 0, unroll=8)

    def wait(r, carry):
        for c in range(2):
            _row_copy(h_ref, r, xs_hbm, pos_ref[0, c, r], sem).wait()
        return carry

    lax.fori_loop(0, MOVE_TILE, wait, 0, unroll=8)


def _dispatch(pos3, h2, xs_zero):
    n, D = h2.shape
    any_spec = pl.BlockSpec(memory_space=pl.ANY)
    return pl.pallas_call(
        _dispatch_kernel,
        grid=(n // MOVE_TILE,),
        in_specs=[pl.BlockSpec((1, 2, MOVE_TILE), lambda i: (i, 0, 0), memory_space=pltpu.SMEM),
                  pl.BlockSpec((MOVE_TILE, D), lambda i: (i, 0)), any_spec],
        out_specs=any_spec,
        out_shape=jax.ShapeDtypeStruct(xs_zero.shape, xs_zero.dtype),
        scratch_shapes=[pltpu.SemaphoreType.DMA(())],
        input_output_aliases={2: 0},
        compiler_params=_cparams("arbitrary"),
        name="moe_dispatch",
    )(pos3, h2, xs_zero)


def _combine_kernel(pos_ref, y_hbm, h_ref, wt_ref, g_ref, b_ref, o_ref, ybuf, sem):
    def start(r, carry):
        for c in range(2):
            _row_copy(y_hbm, pos_ref[0, c, r], ybuf.at[c], r, sem).start()
        return carry

    lax.fori_loop(0, MOVE_TILE, start, 0, unroll=8)

    def wait(r, carry):
        for c in range(2):
            _row_copy(y_hbm, pos_ref[0, c, r], ybuf.at[c], r, sem).wait()
        return carry

    lax.fori_loop(0, MOVE_TILE, wait, 0, unroll=8)

    wt = wt_ref[...]
    f = wt[:, 0:1] * ybuf[0] + wt[:, 1:2] * ybuf[1]
    o_ref[...] = _layer_norm(ALPHA * h_ref[...] + f, g_ref[...], b_ref[...])


def _combine(pos3, y, h2, wt_col, g, b):
    n, D = h2.shape
    tm = MOVE_TILE
    return pl.pallas_call(
        _combine_kernel,
        grid=(n // tm,),
        in_specs=[pl.BlockSpec((1, 2, tm), lambda i: (i, 0, 0), memory_space=pltpu.SMEM),
                  pl.BlockSpec(memory_space=pl.ANY),
                  pl.BlockSpec((tm, D), lambda i: (i, 0)),
                  pl.BlockSpec((tm, 2), lambda i: (i, 0)),
                  _const_spec((1, D)), _const_spec((1, D))],
        out_specs=pl.BlockSpec((tm, D), lambda i: (i, 0)),
        out_shape=jax.ShapeDtypeStruct((n, D), F32),
        scratch_shapes=[pltpu.VMEM((2, tm, D), F32), pltpu.SemaphoreType.DMA(())],
        compiler_params=_cparams("arbitrary"),
        name="moe_combine",
    )(pos3, y, h2, wt_col, g, b)


def _moe(h2, w_router_t, wgu, wd, g, b):
    n, D = h2.shape
    tm = FFN_TILE
    idx, wt = _router(h2, w_router_t)
    rank, cnt = _rank(idx)

    counts = cnt[:, 0]
    padded = (counts + tm - 1) // tm * tm
    ends = jnp.cumsum(padded)
    starts = ends - padded
    one_hot = idx[..., None] == jnp.arange(N_EXPERTS, dtype=jnp.int32)
    pos = jnp.sum(jnp.where(one_hot, starts, 0), axis=-1) + rank
    n_tiles = (2 * n) // tm + N_EXPERTS
    n_used = (ends[-1] // tm).astype(jnp.int32)
    tile_start = jnp.arange(n_tiles, dtype=jnp.int32) * tm
    tile_expert = jnp.sum(tile_start[:, None] >= ends[None, :], axis=1).astype(jnp.int32)
    tile_expert = jnp.minimum(tile_expert, tile_expert[jnp.maximum(n_used - 1, 0)])
    pos3 = pos.reshape(2, n // MOVE_TILE, MOVE_TILE).transpose(1, 0, 2)

    xs = _dispatch(pos3, h2, jnp.zeros((n_tiles * tm, D), F32))
    y = _ffn_grouped(tile_expert, n_used.reshape(1), xs, wgu, wd)
    return _combine(pos3, y, h2, wt.T, g, b)


def _pack_ffn(wg, wu, wd):
    lead = wg.shape[:-2]
    split = lambda w: w.astype(BF16).reshape(*lead, D_MODEL, N_FF_CHUNKS, FF_CHUNK)
    wgu = jnp.concatenate([split(wg), split(wu)], axis=-1)
    wgu = jnp.moveaxis(wgu, -2, -3)
    return wgu, wd.astype(BF16).reshape(*lead, N_FF_CHUNKS, FF_CHUNK, D_MODEL)


def _pack_lru_gates(w_r, w_i):
    def pair(w):
        w = w.reshape(LRU_HEADS // 2, 2, LRU_BLOCK, LRU_BLOCK)
        z = jnp.zeros_like(w[:, 0])
        top = jnp.concatenate([w[:, 0], z], axis=-1)
        bot = jnp.concatenate([z, w[:, 1]], axis=-1)
        return jnp.concatenate([top, bot], axis=-2)
    return jnp.concatenate([pair(w_r), pair(w_i)], axis=-1).astype(BF16)


def kernel(x, meta_tokens, ln_in_g, ln_in_b, w_in, b_gate, da_lambda, da_subln_g, conv_w, conv_b, lru_wr, lru_br, lru_wi, lru_bi, lru_lambda, w_attn_out, w_lru_out, w_o, ln_g, ln_b, ffn_wg, ffn_wu, ffn_wd, router_w, moe_wg, moe_wu, moe_wd):
    B, S, D = x.shape
    tp = S + META_BLOCK
    n = B * tp
    row = lambda a: a.reshape(1, -1)

    meta_pad = jnp.pad(meta_tokens, ((META_ROW0, 0), (0, 0)))
    h = _ln_in(x, meta_pad, row(ln_in_g), row(ln_in_b)).reshape(n, D)

    w_in_b = w_in.astype(BF16)
    wa_b, wl_b, wo_b = w_attn_out.astype(BF16), w_lru_out.astype(BF16), w_o.astype(BF16)
    dense_wgu, dense_wd = _pack_ffn(ffn_wg, ffn_wu, ffn_wd)
    moe_wgu, moe_wdn = _pack_ffn(moe_wg, moe_wu, moe_wd)

    for i in range(DEPTH):
        q, k, v, xr, gr, ga, gl = _in_proj(h, w_in_b[i])
        seq3 = lambda a: a.reshape(B, tp, a.shape[-1])
        ao = _attention(seq3(q), seq3(k), seq3(v), da_lambda[i], row(jnp.tile(da_subln_g[i], 2)), i, S)
        rec = _conv_lru(seq3(xr), seq3(gr), conv_w[i], row(conv_b[i]), _pack_lru_gates(lru_wr[i], lru_wi[i]),
                        row(lru_br[i]), row(lru_bi[i]), row(lru_lambda[i]))
        h = _mix_out(ao.reshape(n, DA_WIDTH), rec.reshape(n, D), ga, gl, h, wa_b[i], wl_b[i], wo_b[i],
                     b_gate[i], row(ln_g[i, 0]), row(ln_b[i, 0]))
        if i % 2 == 0:
            h = _ffn_dense(h, dense_wgu[i // 2], dense_wd[i // 2], row(ln_g[i, 1]), row(ln_b[i, 1]))
        else:
            h = _moe(h, router_w[i // 2].T, moe_wgu[i // 2], moe_wdn[i // 2], row(ln_g[i, 1]), row(ln_b[i, 1]))
    return h.reshape(B, tp, D)[:, :S]
```

```python
import functools
import math

import jax
import jax.numpy as jnp
from jax import lax
from jax.experimental import pallas as pl
from jax.experimental.pallas import tpu as pltpu

D_MODEL = 1024
DEPTH = 4
CHUNK = 64
N_META = 16
DA_HEADS = 8
DA_HEAD_DIM = 32
DA_WIDTH = 512
LRU_WIDTH = 1024
LRU_HEADS = 16
LRU_BLOCK = 64
CONV_WIDTH = 4
LRU_C = 8.0
D_FF = 2816
N_EXPERTS = 8
ALPHA = (2 * DEPTH) ** 0.25
LN_EPS = 1e-5
RMS_EPS = 1e-5
PROJ_WIDTH = 3 * DA_WIDTH + 2 * LRU_WIDTH + 2 * D_MODEL

LANES = 128
META_BLOCK = 128
META_ROW0 = META_BLOCK - N_META
TIME_BLOCK = 128
ATTN_Q = 256
LRU_SEG = 12
LRU_ROWS = 8 * LRU_SEG
FF_CHUNK = 1408
N_FF_CHUNKS = D_FF // FF_CHUNK
ROW_TILE = 512
FFN_TILE = 512
ROUTE_TILE = 512
MOVE_TILE = 256
NEG_BIG = -1e30
VMEM_LIMIT = 56 * 1024 * 1024

BF16 = jnp.bfloat16
F32 = jnp.float32


def _cparams(*sem):
    return pltpu.CompilerParams(dimension_semantics=sem, vmem_limit_bytes=VMEM_LIMIT)


def _layer_norm(x, g, b):
    mu = jnp.mean(x, axis=-1, keepdims=True)
    xc = x - mu
    var = jnp.mean(xc * xc, axis=-1, keepdims=True)
    return xc * lax.rsqrt(var + LN_EPS) * g + b


def _const_spec(shape):
    nd = len(shape)
    return pl.BlockSpec(shape, lambda *_: (0,) * nd)


def _ln_in_kernel(x_ref, meta_ref, g_ref, b_ref, o_ref):
    j = pl.program_id(1)
    last = pl.num_programs(1) - 1

    @pl.when(j < last)
    def _():
        o_ref[0] = _layer_norm(x_ref[0], g_ref[...], b_ref[...])

    @pl.when(j == last)
    def _():
        y = _layer_norm(meta_ref[...], g_ref[...], b_ref[...])
        row = lax.broadcasted_iota(jnp.int32, y.shape, 0)
        o_ref[0] = jnp.where(row >= META_ROW0, y, 0.0)


def _ln_in(x, meta_pad, g, b):
    B, S, D = x.shape
    nblk = S // TIME_BLOCK + 1
    return pl.pallas_call(
        _ln_in_kernel,
        grid=(B, nblk),
        in_specs=[
            pl.BlockSpec((1, TIME_BLOCK, D), lambda b_, j: (b_, jnp.minimum(j, nblk - 2), 0)),
            _const_spec((META_BLOCK, D)),
            _const_spec((1, D)),
            _const_spec((1, D)),
        ],
        out_specs=pl.BlockSpec((1, TIME_BLOCK, D), lambda b_, j: (b_, j, 0)),
        out_shape=jax.ShapeDtypeStruct((B, S + META_BLOCK, D), F32),
        compiler_params=_cparams("parallel", "arbitrary"),
        name="ln_in",
    )(x, meta_pad, g, b)


_PROJ_SEGS = (DA_WIDTH, DA_WIDTH, DA_WIDTH, LRU_WIDTH, LRU_WIDTH, D_MODEL, D_MODEL)


def _in_proj_kernel(h_ref, w_ref, q_ref, k_ref, v_ref, xr_ref, gr_ref, ga_ref, gl_ref):
    hb = h_ref[...].astype(BF16)
    outs = (q_ref, k_ref, v_ref, xr_ref, gr_ref, ga_ref, gl_ref)
    col = 0
    for idx, (o_ref, width) in enumerate(zip(outs, _PROJ_SEGS)):
        for c in range(0, width, 512):
            z = jnp.dot(hb, w_ref[:, col + c:col + c + 512], preferred_element_type=F32)
            if idx == 0:
                z = z * (DA_HEAD_DIM ** -0.5 * math.log2(math.e))
            o_ref[:, c:c + 512] = z.astype(BF16)
        col += width


def _in_proj(h2, w_in):
    n, D = h2.shape
    tm = ROW_TILE
    row = lambda w: pl.BlockSpec((tm, w), lambda i: (i, 0))
    return pl.pallas_call(
        _in_proj_kernel,
        grid=(n // tm,),
        in_specs=[row(D), _const_spec((D, PROJ_WIDTH))],
        out_specs=[row(w) for w in _PROJ_SEGS],
        out_shape=[jax.ShapeDtypeStruct((n, w), BF16) for w in _PROJ_SEGS],
        compiler_params=_cparams("parallel"),
        name="in_proj",
    )(h2, w_in)


_NT = (((1,), (1,)), ((), ()))
HEAD_LANES = 2 * DA_HEAD_DIM


ATTN_PAIRS = 1
ATTN_STREAMS = 2 * ATTN_PAIRS


def _pair_lanes(st):
    pr = st // 2
    return slice(pr * LANES, (pr + 1) * LANES)


def _attn_stack_queries(q, qs_scr):
    tq = q.shape[0]
    lane = lax.broadcasted_iota(jnp.int32, (tq, LANES), 1)
    for st in range(ATTN_STREAMS):
        qp = q[:, _pair_lanes(st)]
        for m in range(2):
            keep = lane // DA_HEAD_DIM == 2 * (st % 2) + m
            qs_scr[st, m * tq:(m + 1) * tq, :] = jnp.where(keep, qp, jnp.zeros_like(qp))


def _attn_scores(qs_scr, st, ks):
    return lax.dot_general(qs_scr[st], ks[:, _pair_lanes(st)], _NT, preferred_element_type=F32)


def _row_max(s):
    return jnp.broadcast_to(jnp.max(s, axis=1, keepdims=True), (s.shape[0], LANES))


def _attn_update(m_scr, acc_scr, scores, vs, mask, first, row_max=None):
    tk = vs.shape[0]
    low = lax.broadcasted_iota(jnp.int32, (tk, LANES), 1) < HEAD_LANES
    for st in range(ATTN_STREAMS):
        vp = vs[:, _pair_lanes(st)]
        own = low if st % 2 == 0 else jnp.logical_not(low)
        rhs = jnp.where(own, vp, jnp.ones_like(vp))
        s = scores[st]
        if mask is not None:
            s = jnp.where(mask, s, NEG_BIG)
        m_cur = _row_max(s) if row_max is None else row_max[st]
        if first:
            m_new = m_cur
        else:
            m_prev = m_scr[st]
            m_new = jnp.maximum(m_prev, m_cur)
        p = jnp.exp2(s - jnp.concatenate([m_new] * (tk // LANES), axis=1)).astype(BF16)
        pv = jnp.dot(p, rhs, preferred_element_type=F32)
        if first:
            acc_scr[st] = pv
        else:
            acc_scr[st] = acc_scr[st] * jnp.exp2(m_prev - m_new) + pv
        m_scr[st] = m_new


def _attn_finish(lam_ref, g_ref, acc_scr, o_ref, lam_init):
    tq = o_ref.shape[1]
    lf = lam_ref[...]
    lam = (jnp.exp(jnp.sum(lf[0:1] * lf[1:2], axis=1, keepdims=True))
           - jnp.exp(jnp.sum(lf[2:3] * lf[3:4], axis=1, keepdims=True)) + lam_init)
    low = lax.broadcasted_iota(jnp.int32, (tq, LANES), 1) < HEAD_LANES
    for pr in range(ATTN_PAIRS):
        diff = []
        for x in range(2):
            acc = acc_scr[2 * pr + x]
            o = acc / pltpu.roll(acc, HEAD_LANES, axis=1)
            diff.append(o[:tq] - lam * o[tq:])
        d = jnp.where(low, diff[0], diff[1])
        dd = d * d
        ss_lo = jnp.sum(jnp.where(low, dd, 0.0), axis=1, keepdims=True)
        ss_hi = jnp.sum(jnp.where(low, 0.0, dd), axis=1, keepdims=True)
        ms = jnp.where(low, ss_lo, ss_hi) * (1.0 / HEAD_LANES)
        out = d * lax.rsqrt(ms + RMS_EPS) * g_ref[...] * (1.0 - lam_init)
        o_ref[0, :, pr * LANES:(pr + 1) * LANES] = out.astype(o_ref.dtype)


def _meta_key_mask(rows):
    return lax.broadcasted_iota(jnp.int32, (rows, META_BLOCK), 1) >= META_ROW0


def _attn_kernel(lam_ref, g_ref, q_ref, k_ref, v_ref, o_in_ref, o_ref, qs_scr, m_scr, acc_scr,
                 sa_scr, sb_scr, ma_scr, mb_scr, *, seq, lam_init):
    del o_in_ref
    i = pl.program_id(2)
    tq = ATTN_Q
    streams = range(ATTN_STREAMS)
    _attn_stack_queries(q_ref[0], qs_scr)
    keys = lambda j: k_ref[0, pl.ds(pl.multiple_of(j * tq, tq), tq), :]
    vals = lambda j: v_ref[0, pl.ds(pl.multiple_of(j * tq, tq), tq), :]
    even = (sa_scr, ma_scr)
    odd = (sb_scr, mb_scr)

    def prefetch(j, slot):
        s_scr, mx_scr = slot
        ks = keys(j)
        for st in streams:
            s = _attn_scores(qs_scr, st, ks)
            s_scr[st] = s
            mx_scr[st] = _row_max(s)

    prefetch(0, even)

    k0 = k_ref[0, seq:seq + META_BLOCK, :]
    _attn_update(m_scr, acc_scr, [_attn_scores(qs_scr, st, k0) for st in streams],
                 v_ref[0, seq:seq + META_BLOCK, :], _meta_key_mask(2 * tq), first=True)

    def full_block(j, cur, nxt):
        prefetch(j + 1, nxt)
        s_scr, mx_scr = cur
        _attn_update(m_scr, acc_scr, [s_scr[st] for st in streams], vals(j), None, first=False,
                     row_max=[mx_scr[st] for st in streams])

    def diagonal_block(cur):
        row = lax.broadcasted_iota(jnp.int32, (2 * tq, tq), 0) % tq
        col = lax.broadcasted_iota(jnp.int32, (2 * tq, tq), 1)
        _attn_update(m_scr, acc_scr, [cur[0][st] for st in streams], vals(i), col // CHUNK <= row // CHUNK,
                     first=False)

    def body(jj, carry):
        full_block(2 * jj, even, odd)
        full_block(2 * jj + 1, odd, even)
        return carry

    lax.fori_loop(0, i // 2, body, 0)

    @pl.when(i % 2 == 1)
    def _():
        full_block(i - 1, even, odd)
        diagonal_block(odd)

    @pl.when(i % 2 == 0)
    def _():
        diagonal_block(even)

    _attn_finish(lam_ref, g_ref, acc_scr, o_ref, lam_init)


def _attn_meta_kernel(lam_ref, g_ref, q_ref, k_ref, v_ref, o_in_ref, o_ref, qs_scr, m_scr, acc_scr, *, lam_init):
    del o_in_ref
    _attn_stack_queries(q_ref[0], qs_scr)
    k0 = k_ref[0]
    _attn_update(m_scr, acc_scr, [_attn_scores(qs_scr, st, k0) for st in range(ATTN_STREAMS)], v_ref[0],
                 _meta_key_mask(2 * META_BLOCK), first=True)
    _attn_finish(lam_ref, g_ref, acc_scr, o_ref, lam_init)


def _attn_scratch(tq):
    state = lambda dt: pltpu.VMEM((ATTN_STREAMS, 2 * tq, LANES), dt)
    return [state(BF16), state(F32), state(F32)]


def _attention(q, k, v, da_lambda, subln_g2, layer, seq):
    B, tp, _ = q.shape
    lam_init = 0.8 - 0.6 * math.exp(-0.3 * layer)
    width = ATTN_PAIRS * LANES
    groups = DA_WIDTH // width
    small = [_const_spec((4, DA_HEAD_DIM)), _const_spec((1, LANES))]
    kv_spec = pl.BlockSpec((1, tp, width), lambda b_, g, i: (b_, 0, g))
    qo_spec = pl.BlockSpec((1, ATTN_Q, width), lambda b_, g, i: (b_, i, g))
    o = pl.pallas_call(
        functools.partial(_attn_kernel, seq=seq, lam_init=lam_init),
        grid=(B, groups, seq // ATTN_Q),
        in_specs=small + [qo_spec, kv_spec, kv_spec, pl.BlockSpec(memory_space=pl.ANY)],
        out_specs=qo_spec,
        out_shape=jax.ShapeDtypeStruct((B, tp, DA_WIDTH), BF16),
        scratch_shapes=(_attn_scratch(ATTN_Q) + [pltpu.VMEM((ATTN_STREAMS, 2 * ATTN_Q, ATTN_Q), F32)] * 2
                        + [pltpu.VMEM((ATTN_STREAMS, 2 * ATTN_Q, LANES), F32)] * 2),
        input_output_aliases={5: 0},
        compiler_params=_cparams("parallel", "parallel", "arbitrary"),
        name="diff_attn",
    )(da_lambda, subln_g2, q, k, v, jnp.zeros((B, tp, DA_WIDTH), BF16))
    meta_spec = pl.BlockSpec((1, META_BLOCK, width), lambda b_, g: (b_, seq // META_BLOCK, g))
    return pl.pallas_call(
        functools.partial(_attn_meta_kernel, lam_init=lam_init),
        grid=(B, groups),
        in_specs=small + [meta_spec, meta_spec, meta_spec, pl.BlockSpec(memory_space=pl.ANY)],
        out_specs=meta_spec,
        out_shape=jax.ShapeDtypeStruct((B, tp, DA_WIDTH), BF16),
        scratch_shapes=_attn_scratch(META_BLOCK),
        input_output_aliases={5: 0},
        compiler_params=_cparams("parallel", "parallel"),
        name="diff_attn_meta",
    )(da_lambda, subln_g2, q, k, v, o)


def _lru_kernel(xr_ref, gr_ref, cw_ref, cb_ref, wg_ref, br_ref, bi_ref, lam_ref, o_ref,
                x_buf, h_buf, tail, carry):
    j = pl.program_id(1)
    seg = LRU_SEG
    seg_rows = lambda t: pl.ds(t, 8, stride=seg)

    @pl.when(j == 0)
    def _():
        tail[...] = jnp.zeros_like(tail)
        carry[...] = jnp.zeros_like(carry)

    sub = lax.broadcasted_iota(jnp.int32, (8, LANES), 0)
    live = [(sub * seg + t >= LRU_ROWS - N_META) | (j > 0) for t in range(seg)]

    for g in range(LRU_WIDTH // LANES):
        sl = slice(g * LANES, (g + 1) * LANES)
        x_buf[g] = xr_ref[0, :, sl].astype(F32)
    for g in range(LRU_WIDTH // LANES):
        sl = slice(g * LANES, (g + 1) * LANES)
        xs = [jnp.where(live[t], x_buf[g, seg_rows(t), :], 0.0) for t in range(seg)]
        before = {}
        for d in range(1, CONV_WIDTH):
            before[-d] = jnp.where(sub == 0, pltpu.roll(tail[d - 1, :, sl], 1, axis=0),
                                   pltpu.roll(xs[seg - d], 1, axis=0))
            tail[d - 1, :, sl] = xs[seg - d]
        at = lambda t: xs[t] if t >= 0 else before[t]
        xc = []
        for t in range(seg):
            y = cb_ref[:, sl] + at(t - CONV_WIDTH + 1) * cw_ref[0:1, sl]
            for tap in range(1, CONV_WIDTH):
                y = y + at(t - CONV_WIDTH + 1 + tap) * cw_ref[tap:tap + 1, sl]
            xc.append(y)
        xc = jnp.concatenate(xc, axis=0)
        lam = lam_ref[:, sl]
        softplus_neg = jnp.maximum(-lam, 0.0) + jnp.log1p(jnp.exp(-jnp.abs(lam)))
        pre = jnp.dot(xc.astype(BF16), wg_ref[g], preferred_element_type=F32)
        r = jax.nn.sigmoid(pre[:, :LANES] + br_ref[:, sl])
        ig = jax.nn.sigmoid(pre[:, LANES:] + bi_ref[:, sl])
        log_a = -LRU_C * r * softplus_neg
        a = jnp.exp(log_a)
        th = jnp.tanh(log_a)
        u = jnp.sqrt(-2.0 * th / (1.0 - th)) * ig * xc

        h = jnp.zeros((8, LANES), F32)
        p = jnp.ones((8, LANES), F32)
        hs, ps = [], []
        for t in range(seg):
            a_t = a[8 * t:8 * t + 8]
            h = a_t * h + jnp.where(live[t], u[8 * t:8 * t + 8], 0.0)
            p = a_t * p
            hs.append(h)
            ps.append(p)
        c = carry[0:1, sl]
        inits = []
        for s in range(8):
            inits.append(c)
            c = h[s:s + 1, :] + p[s:s + 1, :] * c
        carry[:, sl] = jnp.broadcast_to(c, (8, LANES))
        init = jnp.concatenate(inits, axis=0)
        for t in range(seg):
            h_buf[g, seg_rows(t), :] = hs[t] + ps[t] * init
        o_ref[0, :, sl] = (h_buf[g] * jax.nn.gelu(gr_ref[0, :, sl].astype(F32))).astype(o_ref.dtype)


def _conv_lru(xr, gr, conv_w, conv_b, w_gate, b_r, b_i, lru_lambda):
    B, tp, C = xr.shape
    assert tp % LRU_ROWS == 0, (tp, LRU_ROWS)
    nblk = tp // LRU_ROWS
    blk = pl.BlockSpec((1, LRU_ROWS, C), lambda b_, j: (b_, (j + nblk - 1) % nblk, 0))
    vm = lambda r: pltpu.VMEM((r, C), F32)
    grp = pltpu.VMEM((C // LANES, LRU_ROWS, LANES), F32)
    return pl.pallas_call(
        _lru_kernel,
        grid=(B, nblk),
        in_specs=[blk, blk, _const_spec((CONV_WIDTH, C)), _const_spec((1, C)),
                  _const_spec((C // LANES, LANES, 2 * LANES)), _const_spec((1, C)), _const_spec((1, C)),
                  _const_spec((1, C))],
        out_specs=blk,
        out_shape=jax.ShapeDtypeStruct((B, tp, C), BF16),
        scratch_shapes=[grp, grp, pltpu.VMEM((CONV_WIDTH - 1, 8, C), F32), vm(8)],
        compiler_params=_cparams("parallel", "arbitrary"),
        name="conv_rglru",
    )(xr, gr, conv_w, conv_b, w_gate, b_r, b_i, lru_lambda)


def _mix_out_kernel(ao_ref, rec_ref, ga_ref, gl_ref, h_ref, wa_ref, wl_ref, wo_ref, bg_ref, g_ref, b_ref, o_ref):
    attn_up = jnp.dot(ao_ref[...], wa_ref[...], preferred_element_type=F32)
    lru_up = jnp.dot(rec_ref[...], wl_ref[...], preferred_element_type=F32)
    merged = (jax.nn.sigmoid(ga_ref[...].astype(F32) + bg_ref[0:1, :]) * attn_up
              + jax.nn.sigmoid(gl_ref[...].astype(F32) + bg_ref[1:2, :]) * lru_up)
    mix = jnp.dot(merged.astype(BF16), wo_ref[...], preferred_element_type=F32)
    o_ref[...] = _layer_norm(ALPHA * h_ref[...] + mix, g_ref[...], b_ref[...])


def _mix_out(ao, rec, ga, gl, h2, wa, wl, wo, bg, g, b):
    n, D = h2.shape
    tm = ROW_TILE
    row = lambda w: pl.BlockSpec((tm, w), lambda i: (i, 0))
    return pl.pallas_call(
        _mix_out_kernel,
        grid=(n // tm,),
        in_specs=[row(DA_WIDTH), row(D), row(D), row(D), row(D),
                  _const_spec((DA_WIDTH, D)), _const_spec((D, D)), _const_spec((D, D)),
                  _const_spec((2, D)), _const_spec((1, D)), _const_spec((1, D))],
        out_specs=row(D),
        out_shape=jax.ShapeDtypeStruct((n, D), F32),
        compiler_params=_cparams("parallel"),
        name="mix_out",
    )(ao, rec, ga, gl, h2, wa, wl, wo, bg, g, b)


def _swiglu_chunks(j, x_ref, wgu_ref, wd_ref, xb_scr, acc_scr, finish, enabled=True):
    last = N_FF_CHUNKS - 1

    def part():
        gu = jnp.dot(xb_scr[...], wgu_ref[...], preferred_element_type=F32)
        act = (jax.nn.silu(gu[:, :FF_CHUNK]) * gu[:, FF_CHUNK:]).astype(BF16)
        return jnp.dot(act, wd_ref[...], preferred_element_type=F32)

    @pl.when(enabled & (j == 0))
    def _():
        xb_scr[...] = x_ref[...].astype(BF16)
        acc_scr[...] = part()

    if N_FF_CHUNKS > 2:
        @pl.when(enabled & (j > 0) & (j < last))
        def _():
            acc_scr[...] += part()

    @pl.when(enabled & (j == last))
    def _():
        finish(acc_scr[...] + part())


def _ffn_dense_kernel(x_ref, wgu_ref, wd_ref, g_ref, b_ref, o_ref, xb_scr, acc_scr):
    def finish(f):
        o_ref[...] = _layer_norm(ALPHA * x_ref[...] + f, g_ref[...], b_ref[...])

    _swiglu_chunks(pl.program_id(1), x_ref, wgu_ref.at[0], wd_ref.at[0], xb_scr, acc_scr, finish)


def _ffn_dense(h2, wgu, wd, g, b):
    n, D = h2.shape
    tm = FFN_TILE
    return pl.pallas_call(
        _ffn_dense_kernel,
        grid=(n // tm, N_FF_CHUNKS),
        in_specs=[pl.BlockSpec((tm, D), lambda i, j: (i, 0)),
                  pl.BlockSpec((1, D, 2 * FF_CHUNK), lambda i, j: (j, 0, 0)),
                  pl.BlockSpec((1, FF_CHUNK, D), lambda i, j: (j, 0, 0)),
                  _const_spec((1, D)), _const_spec((1, D))],
        out_specs=pl.BlockSpec((tm, D), lambda i, j: (i, 0)),
        out_shape=jax.ShapeDtypeStruct((n, D), F32),
        scratch_shapes=[pltpu.VMEM((tm, D), BF16), pltpu.VMEM((tm, D), F32)],
        compiler_params=_cparams("parallel", "arbitrary"),
        name="ffn_dense",
    )(h2, wgu, wd, g, b)


def _ffn_group_kernel(te_ref, nu_ref, x_ref, wgu_ref, wd_ref, o_ref, xb_scr, acc_scr):
    t = pl.program_id(0)
    j = pl.program_id(1)
    used = t < nu_ref[0]

    def finish(f):
        o_ref[...] = f

    _swiglu_chunks(j, x_ref, wgu_ref.at[0, 0], wd_ref.at[0, 0], xb_scr, acc_scr, finish, enabled=used)

    @pl.when(jnp.logical_not(used) & (j == N_FF_CHUNKS - 1))
    def _():
        o_ref[...] = jnp.zeros_like(o_ref)


def _ffn_grouped(tile_expert, n_used, xs, wgu, wd):
    n, D = xs.shape
    tm = FFN_TILE
    nf = N_FF_CHUNKS

    def chunk(t, j, te, nu):
        return jnp.where(t < nu[0], j, nf - 1)

    grid_spec = pltpu.PrefetchScalarGridSpec(
        num_scalar_prefetch=2,
        grid=(n // tm, nf),
        in_specs=[pl.BlockSpec((tm, D), lambda t, j, te, nu: (jnp.minimum(t, nu[0] - 1), 0)),
                  pl.BlockSpec((1, 1, D, 2 * FF_CHUNK), lambda t, j, te, nu: (te[t], chunk(t, j, te, nu), 0, 0)),
                  pl.BlockSpec((1, 1, FF_CHUNK, D), lambda t, j, te, nu: (te[t], chunk(t, j, te, nu), 0, 0))],
        out_specs=pl.BlockSpec((tm, D), lambda t, j, te, nu: (t, 0)),
        scratch_shapes=[pltpu.VMEM((tm, D), BF16), pltpu.VMEM((tm, D), F32)],
    )
    return pl.pallas_call(
        _ffn_group_kernel,
        grid_spec=grid_spec,
        out_shape=jax.ShapeDtypeStruct((n, D), F32),
        compiler_params=_cparams("arbitrary", "arbitrary"),
        name="ffn_grouped",
    )(tile_expert, n_used, xs, wgu, wd)


def _router_kernel(h_ref, w_ref, idx_ref, wt_ref):
    nt = (((1,), (1,)), ((), ()))
    h = h_ref[...]
    w = w_ref[...]
    h_hi = h.astype(BF16)
    h_lo = (h - h_hi.astype(F32)).astype(BF16)
    w_hi = w.astype(BF16)
    w_lo = (w - w_hi.astype(F32)).astype(BF16)
    logits = (lax.dot_general(w_hi, h_hi, nt, preferred_element_type=F32)
              + lax.dot_general(w_lo, h_hi, nt, preferred_element_type=F32)
              + lax.dot_general(w_hi, h_lo, nt, preferred_element_type=F32))
    e = lax.broadcasted_iota(jnp.int32, logits.shape, 0)
    v1 = jnp.max(logits, axis=0, keepdims=True)
    i1 = jnp.min(jnp.where(logits == v1, e, N_EXPERTS), axis=0, keepdims=True)
    rest = jnp.where(e == i1, -jnp.inf, logits)
    v2 = jnp.max(rest, axis=0, keepdims=True)
    i2 = jnp.min(jnp.where(rest == v2, e, N_EXPERTS), axis=0, keepdims=True)
    ex = jnp.exp(v2 - v1)
    w1 = 1.0 / (1.0 + ex)
    idx_ref[...] = jnp.concatenate([i1, i2], axis=0)
    wt_ref[...] = jnp.concatenate([w1, ex * w1], axis=0)


def _router(h2, w_t):
    n, D = h2.shape
    tm = ROUTE_TILE
    out = pl.BlockSpec((2, tm), lambda i: (0, i))
    return pl.pallas_call(
        _router_kernel,
        grid=(n // tm,),
        in_specs=[pl.BlockSpec((tm, D), lambda i: (i, 0)), _const_spec((N_EXPERTS, D))],
        out_specs=[out, out],
        out_shape=[jax.ShapeDtypeStruct((2, n), jnp.int32), jax.ShapeDtypeStruct((2, n), F32)],
        compiler_params=_cparams("parallel"),
        name="router_top2",
    )(h2, w_t)


def _rank_kernel(idx_ref, rk_ref, cnt_ref, carry):
    i = pl.program_id(0)
    tm = ROUTE_TILE

    @pl.when(i == 0)
    def _():
        carry[...] = jnp.zeros_like(carry)

    idx = idx_ref[...]
    e = lax.broadcasted_iota(jnp.int32, (N_EXPERTS, tm), 0)
    hit0 = e == idx[0:1, :]
    hit1 = e == idx[1:2, :]
    member = (hit0 | hit1).astype(F32)
    before = (lax.broadcasted_iota(jnp.int32, (tm, tm), 0)
              < lax.broadcasted_iota(jnp.int32, (tm, tm), 1)).astype(BF16)
    rank = jnp.dot(member.astype(BF16), before, preferred_element_type=F32) + carry[:, 0:1]
    r0 = jnp.sum(jnp.where(hit0, rank, 0.0), axis=0, keepdims=True)
    r1 = jnp.sum(jnp.where(hit1, rank, 0.0), axis=0, keepdims=True)
    rk_ref[...] = jnp.concatenate([r0, r1], axis=0).astype(jnp.int32)
    total = carry[...] + jnp.sum(member, axis=1, keepdims=True)
    carry[...] = total
    cnt_ref[...] = total.astype(jnp.int32)


def _rank(idx):
    n = idx.shape[1]
    tm = ROUTE_TILE
    blk = pl.BlockSpec((2, tm), lambda i: (0, i))
    return pl.pallas_call(
        _rank_kernel,
        grid=(n // tm,),
        in_specs=[blk],
        out_specs=[blk, _const_spec((N_EXPERTS, LANES))],
        out_shape=[jax.ShapeDtypeStruct((2, n), jnp.int32), jax.ShapeDtypeStruct((N_EXPERTS, LANES), jnp.int32)],
        scratch_shapes=[pltpu.VMEM((N_EXPERTS, LANES), F32)],
        compiler_params=_cparams("arbitrary"),
        name="route_rank",
    )(idx)


def _row_copy(src, s, dst, d, sem):
    return pltpu.make_async_copy(src.at[pl.ds(s, 1)], dst.at[pl.ds(d, 1)], sem)


def _dispatch_kernel(pos_ref, h_ref, xs_in, xs_hbm, sem):
    del xs_in

    def start(r, carry):
        for c in range(2):
            _row_copy(h_ref, r, xs_hbm, pos_ref[0, c, r], sem).start()
        return carry

    lax.fori_loop(0, MOVE_TILE, start, 0, unroll=8)

    def wait(r, carry):
        for c in range(2):
            _row_copy(h_ref, r, xs_hbm, pos_ref[0, c, r], sem).wait()
        return carry

    lax.fori_loop(0, MOVE_TILE, wait, 0, unroll=8)


def _dispatch(pos3, h2, xs_zero):
    n, D = h2.shape
    any_spec = pl.BlockSpec(memory_space=pl.ANY)
    return pl.pallas_call(
        _dispatch_kernel,
        grid=(n // MOVE_TILE,),
        in_specs=[pl.BlockSpec((1, 2, MOVE_TILE), lambda i: (i, 0, 0), memory_space=pltpu.SMEM),
                  pl.BlockSpec((MOVE_TILE, D), lambda i: (i, 0)), any_spec],
        out_specs=any_spec,
        out_shape=jax.ShapeDtypeStruct(xs_zero.shape, xs_zero.dtype),
        scratch_shapes=[pltpu.SemaphoreType.DMA(())],
        input_output_aliases={2: 0},
        compiler_params=_cparams("arbitrary"),
        name="moe_dispatch",
    )(pos3, h2, xs_zero)


def _combine_kernel(pos_ref, y_hbm, h_ref, wt_ref, g_ref, b_ref, o_ref, ybuf, sem):
    def start(r, carry):
        for c in range(2):
            _row_copy(y_hbm, pos_ref[0, c, r], ybuf.at[c], r, sem).start()
        return carry

    lax.fori_loop(0, MOVE_TILE, start, 0, unroll=8)

    def wait(r, carry):
        for c in range(2):
            _row_copy(y_hbm, pos_ref[0, c, r], ybuf.at[c], r, sem).wait()
        return carry

    lax.fori_loop(0, MOVE_TILE, wait, 0, unroll=8)

    wt = wt_ref[...]
    f = wt[:, 0:1] * ybuf[0] + wt[:, 1:2] * ybuf[1]
    o_ref[...] = _layer_norm(ALPHA * h_ref[...] + f, g_ref[...], b_ref[...])


def _combine(pos3, y, h2, wt_col, g, b):
    n, D = h2.shape
    tm = MOVE_TILE
    return pl.pallas_call(
        _combine_kernel,
        grid=(n // tm,),
        in_specs=[pl.BlockSpec((1, 2, tm), lambda i: (i, 0, 0), memory_space=pltpu.SMEM),
                  pl.BlockSpec(memory_space=pl.ANY),
                  pl.BlockSpec((tm, D), lambda i: (i, 0)),
                  pl.BlockSpec((tm, 2), lambda i: (i, 0)),
                  _const_spec((1, D)), _const_spec((1, D))],
        out_specs=pl.BlockSpec((tm, D), lambda i: (i, 0)),
        out_shape=jax.ShapeDtypeStruct((n, D), F32),
        scratch_shapes=[pltpu.VMEM((2, tm, D), F32), pltpu.SemaphoreType.DMA(())],
        compiler_params=_cparams("arbitrary"),
        name="moe_combine",
    )(pos3, y, h2, wt_col, g, b)


def _moe(h2, w_router_t, wgu, wd, g, b):
    n, D = h2.shape
    tm = FFN_TILE
    idx, wt = _router(h2, w_router_t)
    rank, cnt = _rank(idx)

    counts = cnt[:, 0]
    padded = (counts + tm - 1) // tm * tm
    ends = jnp.cumsum(padded)
    starts = ends - padded
    one_hot = idx[..., None] == jnp.arange(N_EXPERTS, dtype=jnp.int32)
    pos = jnp.sum(jnp.where(one_hot, starts, 0), axis=-1) + rank
    n_tiles = (2 * n) // tm + N_EXPERTS
    n_used = (ends[-1] // tm).astype(jnp.int32)
    tile_start = jnp.arange(n_tiles, dtype=jnp.int32) * tm
    tile_expert = jnp.sum(tile_start[:, None] >= ends[None, :], axis=1).astype(jnp.int32)
    tile_expert = jnp.minimum(tile_expert, tile_expert[jnp.maximum(n_used - 1, 0)])
    pos3 = pos.reshape(2, n // MOVE_TILE, MOVE_TILE).transpose(1, 0, 2)

    xs = _dispatch(pos3, h2, jnp.zeros((n_tiles * tm, D), F32))
    y = _ffn_grouped(tile_expert, n_used.reshape(1), xs, wgu, wd)
    return _combine(pos3, y, h2, wt.T, g, b)


def _pack_ffn(wg, wu, wd):
    lead = wg.shape[:-2]
    split = lambda w: w.astype(BF16).reshape(*lead, D_MODEL, N_FF_CHUNKS, FF_CHUNK)
    wgu = jnp.concatenate([split(wg), split(wu)], axis=-1)
    wgu = jnp.moveaxis(wgu, -2, -3)
    return wgu, wd.astype(BF16).reshape(*lead, N_FF_CHUNKS, FF_CHUNK, D_MODEL)


def _pack_lru_gates(w_r, w_i):
    def pair(w):
        w = w.reshape(LRU_HEADS // 2, 2, LRU_BLOCK, LRU_BLOCK)
        z = jnp.zeros_like(w[:, 0])
        top = jnp.concatenate([w[:, 0], z], axis=-1)
        bot = jnp.concatenate([z, w[:, 1]], axis=-1)
        return jnp.concatenate([top, bot], axis=-2)
    return jnp.concatenate([pair(w_r), pair(w_i)], axis=-1).astype(BF16)


def kernel(x, meta_tokens, ln_in_g, ln_in_b, w_in, b_gate, da_lambda, da_subln_g, conv_w, conv_b, lru_wr, lru_br, lru_wi, lru_bi, lru_lambda, w_attn_out, w_lru_out, w_o, ln_g, ln_b, ffn_wg, ffn_wu, ffn_wd, router_w, moe_wg, moe_wu, moe_wd):
    B, S, D = x.shape
    tp = S + META_BLOCK
    n = B * tp
    row = lambda a: a.reshape(1, -1)

    meta_pad = jnp.pad(meta_tokens, ((META_ROW0, 0), (0, 0)))
    h = _ln_in(x, meta_pad, row(ln_in_g), row(ln_in_b)).reshape(n, D)

    w_in_b = w_in.astype(BF16)
    wa_b, wl_b, wo_b = w_attn_out.astype(BF16), w_lru_out.astype(BF16), w_o.astype(BF16)
    dense_wgu, dense_wd = _pack_ffn(ffn_wg, ffn_wu, ffn_wd)
    moe_wgu, moe_wdn = _pack_ffn(moe_wg, moe_wu, moe_wd)

    for i in range(DEPTH):
        q, k, v, xr, gr, ga, gl = _in_proj(h, w_in_b[i])
        seq3 = lambda a: a.reshape(B, tp, a.shape[-1])
        ao = _attention(seq3(q), seq3(k), seq3(v), da_lambda[i], row(jnp.tile(da_subln_g[i], 2)), i, S)
        rec = _conv_lru(seq3(xr), seq3(gr), conv_w[i], row(conv_b[i]), _pack_lru_gates(lru_wr[i], lru_wi[i]),
                        row(lru_br[i]), row(lru_bi[i]), row(lru_lambda[i]))
        h = _mix_out(ao.reshape(n, DA_WIDTH), rec.reshape(n, D), ga, gl, h, wa_b[i], wl_b[i], wo_b[i],
                     b_gate[i], row(ln_g[i, 0]), row(ln_b[i, 0]))
        if i % 2 == 0:
            h = _ffn_dense(h, dense_wgu[i // 2], dense_wd[i // 2], row(ln_g[i, 1]), row(ln_b[i, 1]))
        else:
            h = _moe(h, router_w[i // 2].T, moe_wgu[i // 2], moe_wdn[i // 2], row(ln_g[i, 1]), row(ln_b[i, 1]))
    return h.reshape(B, tp, D)[:, :S]
```
